```python
import jax, jax.numpy as jnp
from jax import lax
import numpy as np

D_MODEL = 1024
BATCH = 8
SEQ = 2048
DEPTH = 2
DEC_BATCH = 128
DEC_SEQ = 8
PAST_LEN = 2048
PAGE_SIZE = 128

HEAD_DIM = 64
N_A = 4
N_B = 6
N_C = 6
W_A = N_A * HEAD_DIM
W_B = N_B * HEAD_DIM
W_C = N_C * HEAD_DIM
MIX = W_A + W_B + W_C
CHUNK = 128
Q_BLOCK = 128
D_FF = 256 * (-(-8 * D_MODEL // (3 * 256)))
N_IN = 2 * W_A + 3 * W_B + 3 * W_C + N_C
EPS = 1e-6
FORGET_BIAS = 2.0
NEG = -1e30

kernel_name = "hybrid_chunkmlp_stickbreak_fox_decode_step"


def rms_norm(x, g):
    xf = x.astype(jnp.float32)
    y = xf * lax.rsqrt(jnp.mean(xf * xf, axis=-1, keepdims=True) + EPS)
    return (y * g.astype(jnp.float32)).astype(x.dtype)


def split_proj(z, b_f):
    lead = z.shape[:-1]
    sizes = [W_A, W_A, W_B, W_B, W_B, W_C, W_C, W_C, N_C]
    u_a, v_a, q_b, k_b, v_b, q_c, k_c, v_c, f_c = jnp.split(z, np.cumsum(sizes)[:-1].tolist(), axis=-1)
    hb = lambda a: a.reshape(*lead, N_B, HEAD_DIM)
    hc = lambda a: a.reshape(*lead, N_C, HEAD_DIM)
    log_f = jax.nn.log_sigmoid(f_c.astype(jnp.float32) + b_f.astype(jnp.float32))
    return (jax.nn.gelu(u_a), jax.nn.gelu(v_a), hb(q_b), hb(k_b), hb(v_b),
            hc(q_c), hc(k_c), hc(v_c), log_f)


def chunk_spatial_gate(u, v_n, w_s, b_s):
    B, L, _ = u.shape
    n = min(L, CHUNK)
    nc = L // n
    vc = v_n.reshape(B, nc, n, N_A, HEAD_DIM).astype(jnp.float32)
    ws = jnp.where(jnp.tril(jnp.ones((n, n), dtype=bool)), w_s[:, :n, :n].astype(jnp.float32), 0.0)
    mixed = jnp.einsum('gts,bcsgd->bctgd', ws, vc) + b_s[:, :n].astype(jnp.float32).T[None, None, :, :, None]
    out = u.reshape(B, nc, n, N_A, HEAD_DIM) * mixed.astype(u.dtype)
    return out.reshape(B, L, W_A)


def stick_breaking(q, k, v, q_pos, k_pos):
    z = jnp.einsum('bthd,blhd->bhtl', q.astype(jnp.float32), k.astype(jnp.float32)) * (HEAD_DIM ** -0.5)
    mask = k_pos[None, :] < q_pos[:, None]
    log_1m = jnp.where(mask, jax.nn.log_sigmoid(-z), 0.0)
    after = lax.cumsum(log_1m, axis=3, reverse=True) - log_1m
    a = jnp.where(mask, jnp.exp(jax.nn.log_sigmoid(z) + after), 0.0)
    return jnp.einsum('bhtl,blhd->bthd', a, v.astype(jnp.float32)).astype(v.dtype)


def forgetting_attention(q, k, v, fq, fk, q_pos, k_pos):
    z = jnp.einsum('bthd,blhd->bhtl', q.astype(jnp.float32), k.astype(jnp.float32)) * (HEAD_DIM ** -0.5)
    z = z + jnp.transpose(fq, (0, 2, 1))[:, :, :, None] - jnp.transpose(fk, (0, 2, 1))[:, :, None, :]
    mask = k_pos[None, :] <= q_pos[:, None]
    p = jax.nn.softmax(jnp.where(mask, z, NEG), axis=-1)
    return jnp.einsum('bhtl,blhd->bthd', p, v.astype(jnp.float32)).astype(v.dtype)


def sb_prompt(q, k, v):
    B, S = q.shape[:2]
    nb = S // Q_BLOCK
    pos = jnp.arange(S, dtype=jnp.int32)
    qb = q.reshape(B, nb, Q_BLOCK, N_B, HEAD_DIM).swapaxes(0, 1)
    pb = pos.reshape(nb, Q_BLOCK)
    out = lax.map(lambda a: stick_breaking(a[0], k, v, a[1], pos), (qb, pb))
    return out.swapaxes(0, 1).reshape(B, S, N_B, HEAD_DIM)


def fox_prompt(q, k, v, log_f):
    B, S = q.shape[:2]
    nb = S // Q_BLOCK
    F = jnp.cumsum(log_f, axis=1)
    pos = jnp.arange(S, dtype=jnp.int32)
    qb = q.reshape(B, nb, Q_BLOCK, N_C, HEAD_DIM).swapaxes(0, 1)
    fb = F.reshape(B, nb, Q_BLOCK, N_C).swapaxes(0, 1)
    pb = pos.reshape(nb, Q_BLOCK)
    out = lax.map(lambda a: forgetting_attention(a[0], k, v, a[1], F, a[2], pos), (qb, fb, pb))
    return out.swapaxes(0, 1).reshape(B, S, N_C, HEAD_DIM)


def gather_pages(cache, page_table):
    g = cache[page_table]
    return g.reshape(g.shape[0], g.shape[1] * g.shape[2], *g.shape[3:])


def merge_groups(a_out, b_out, c_out, g_mix, w_o):
    lead = a_out.shape[:-1]
    cat = jnp.concatenate([rms_norm(a_out, g_mix[:W_A]),
                           rms_norm(b_out.reshape(*lead, W_B), g_mix[W_A:W_A + W_B]),
                           rms_norm(c_out.reshape(*lead, W_C), g_mix[W_A + W_B:])], axis=-1)
    return cat @ w_o


def swiglu(x, w_in, w_out):
    h = x @ w_in
    return (jax.nn.silu(h[..., :D_FF]) * h[..., D_FF:]) @ w_out


def setup_inputs(seed: int = 0) -> dict:
    key = jax.random.key(seed)
    ks = jax.random.split(key, 24)
    n_pages = PAST_LEN // PAGE_SIZE
    n_pool = (DEC_BATCH * n_pages * 5) // 4
    f32 = jnp.float32
    nrm = lambda k, s, sc=1.0: jax.random.normal(k, s, f32) * sc
    kv_shape_b = (DEPTH, n_pool, PAGE_SIZE, N_B, HEAD_DIM)
    kv_shape_c = (DEPTH, n_pool, PAGE_SIZE, N_C, HEAD_DIM)
    perm = jax.random.permutation(ks[7], n_pool)[:DEC_BATCH * n_pages]
    return {
        "x_prompt": nrm(ks[0], (BATCH, SEQ, D_MODEL)),
        "x_sample": nrm(ks[1], (DEC_BATCH, DEC_SEQ, D_MODEL)),
        "cache_sb_k": nrm(ks[2], kv_shape_b),
        "cache_sb_v": nrm(ks[3], kv_shape_b),
        "cache_fox_k": nrm(ks[4], kv_shape_c),
        "cache_fox_v": nrm(ks[5], kv_shape_c),
        "cache_fox_logf": jax.nn.log_sigmoid(FORGET_BIAS + nrm(ks[6], (DEPTH, n_pool, PAGE_SIZE, N_C))),
        "page_table": perm.reshape(DEC_BATCH, n_pages).astype(jnp.int32),
        "g_attn": 1.0 + nrm(ks[8], (DEPTH, D_MODEL), 0.02),
        "w_in": nrm(ks[9], (DEPTH, D_MODEL, N_IN), D_MODEL ** -0.5),
        "b_f": FORGET_BIAS + nrm(ks[10], (DEPTH, N_C), 0.1),
        "g_v": 1.0 + nrm(ks[11], (DEPTH, W_A), 0.02),
        "w_s": nrm(ks[12], (DEPTH, N_A, CHUNK, CHUNK), CHUNK ** -0.5),
        "b_s": 1.0 + nrm(ks[13], (DEPTH, N_A, CHUNK), 0.1),
        "g_mix": 1.0 + nrm(ks[14], (DEPTH, MIX), 0.02),
        "w_o": nrm(ks[15], (DEPTH, MIX, D_MODEL), MIX ** -0.5),
        "g_ffn": 1.0 + nrm(ks[16], (DEPTH, D_MODEL), 0.02),
        "w_ffn_in": nrm(ks[17], (DEPTH, D_MODEL, 2 * D_FF), D_MODEL ** -0.5),
        "w_ffn_out": nrm(ks[18], (DEPTH, D_FF, D_MODEL), D_FF ** -0.5),
        "g_final": 1.0 + nrm(ks[19], (D_MODEL,), 0.02),
    }


def reference(x_prompt, x_sample, cache_sb_k, cache_sb_v, cache_fox_k, cache_fox_v, cache_fox_logf,
              page_table, g_attn, w_in, b_f, g_v, w_s, b_s, g_mix, w_o, g_ffn, w_ffn_in, w_ffn_out,
              g_final):
    past = page_table.shape[1] * PAGE_SIZE
    T = x_sample.shape[1]
    q_pos = past + jnp.arange(T, dtype=jnp.int32)
    k_pos = jnp.arange(past + T, dtype=jnp.int32)
    yp, ys = x_prompt, x_sample
    p_sbk, p_sbv, p_fk, p_fv, p_lf = [], [], [], [], []
    s_sbk, s_sbv, s_fk, s_fv, s_lf, s_cv = [], [], [], [], [], []
    for l in range(DEPTH):
        u, va, qb, kb, vb, qc, kc, vc, lf = split_proj(rms_norm(yp, g_attn[l]) @ w_in[l], b_f[l])
        va = rms_norm(va, g_v[l])
        a_out = chunk_spatial_gate(u, va, w_s[l], b_s[l])
        b_out = sb_prompt(qb, kb, vb)
        c_out = fox_prompt(qc, kc, vc, lf)
        yp = yp + merge_groups(a_out, b_out, c_out, g_mix[l], w_o[l])
        yp = yp + swiglu(rms_norm(yp, g_ffn[l]), w_ffn_in[l], w_ffn_out[l])
        p_sbk.append(kb); p_sbv.append(vb); p_fk.append(kc); p_fv.append(vc); p_lf.append(lf)
        u, va, qb, kb, vb, qc, kc, vc, lf = split_proj(rms_norm(ys, g_attn[l]) @ w_in[l], b_f[l])
        va = rms_norm(va, g_v[l])
        a_out = chunk_spatial_gate(u, va, w_s[l], b_s[l])
        kb_all = jnp.concatenate([gather_pages(cache_sb_k[l], page_table), kb], axis=1)
        vb_all = jnp.concatenate([gather_pages(cache_sb_v[l], page_table), vb], axis=1)
        b_out = stick_breaking(qb, kb_all, vb_all, q_pos, k_pos)
        kc_all = jnp.concatenate([gather_pages(cache_fox_k[l], page_table), kc], axis=1)
        vc_all = jnp.concatenate([gather_pages(cache_fox_v[l], page_table), vc], axis=1)
        lf_all = jnp.concatenate([gather_pages(cache_fox_logf[l], page_table).astype(jnp.float32), lf], axis=1)
        F = jnp.cumsum(lf_all, axis=1)
        c_out = forgetting_attention(qc, kc_all, vc_all, F[:, past:], F, q_pos, k_pos)
        ys = ys + merge_groups(a_out, b_out, c_out, g_mix[l], w_o[l])
        ys = ys + swiglu(rms_norm(ys, g_ffn[l]), w_ffn_in[l], w_ffn_out[l])
        s_sbk.append(kb); s_sbv.append(vb); s_fk.append(kc); s_fv.append(vc); s_lf.append(lf); s_cv.append(va)
    y_prompt = rms_norm(yp, g_final)
    y_sample = rms_norm(ys, g_final)
    return (y_prompt, y_sample,
            jnp.stack(p_sbk), jnp.stack(p_sbv), jnp.stack(p_fk), jnp.stack(p_fv), jnp.stack(p_lf),
            jnp.stack(s_sbk), jnp.stack(s_sbv), jnp.stack(s_fk), jnp.stack(s_fv), jnp.stack(s_lf),
            jnp.stack(s_cv))
```

```python
import functools
import math

import jax
import jax.numpy as jnp
import numpy as np
from jax import lax
from jax.experimental import pallas as pl
from jax.experimental.pallas import tpu as pltpu

D_MODEL = 1024
HEAD_DIM = 64
N_A, N_B, N_C = 4, 6, 6
W_A, W_B, W_C = N_A * HEAD_DIM, N_B * HEAD_DIM, N_C * HEAD_DIM
CHUNK = 128
PAGE = 128
D_FF = 2816
EPS = 1e-6
NEG = -1e30
LANE = 128
SUBLANE = 8
BLK = 128
VMEM_LIMIT = 48 * 1024 * 1024
BF = jnp.bfloat16
F32 = jnp.float32

_O_U, _O_VA, _O_QB, _O_KB, _O_VB = 0, W_A, 2 * W_A, 2 * W_A + W_B, 2 * W_A + 2 * W_B
_O_QC = 2 * W_A + 3 * W_B
_O_KC, _O_VC, _O_F = _O_QC + W_C, _O_QC + 2 * W_C, _O_QC + 3 * W_C


def _cparams(*sem):
    return pltpu.CompilerParams(dimension_semantics=sem, vmem_limit_bytes=VMEM_LIMIT)


def _gelu(x):
    return 0.5 * x * (1.0 + jnp.tanh(math.sqrt(2.0 / math.pi) * (x + 0.044715 * (x * x * x))))


def _log_sigmoid(x):
    return jnp.minimum(x, 0.0) - jnp.log(1.0 + jnp.exp(-jnp.abs(x)))


def _rms(x, g):
    return x * lax.rsqrt(jnp.mean(x * x, axis=-1, keepdims=True) + EPS) * g


def _dot(a, b):
    return jnp.dot(a, b, preferred_element_type=F32)


def _dot_nt(a, b):
    return lax.dot_general(a, b, (((1,), (1,)), ((), ())), preferred_element_type=F32)


def _inproj_tok_epilogue(zt, bf_col, gv, u_ref, va_ref, qb_ref, qc_ref, lfc_ref):
    u_ref[...] = _gelu(zt[:, 0:W_A])
    va_ref[...] = _rms(_gelu(zt[:, W_A:2 * W_A]), gv)
    qb_ref[...] = zt[:, 2 * W_A:2 * W_A + W_B] * (HEAD_DIM ** -0.5)
    qc_ref[...] = zt[:, 2 * W_A + W_B:2 * W_A + W_B + W_C] * (HEAD_DIM ** -0.5)
    o = 2 * W_A + W_B + W_C
    lfc_ref[...] = _log_sigmoid(zt[:, o:o + LANE] + bf_col)


def _inproj_prompt_kernel(x_ref, g_ref, wt_ref, wT_ref, bfc_ref, bfr_ref, gv_ref,
                          u_ref, va_ref, qb_ref, qc_ref, lfc_ref,
                          kbT_ref, vbT_ref, kcT_ref, vcT_ref, lfr_ref):
    xn = _rms(x_ref[...], g_ref[...]).astype(BF)
    zt = _dot(xn, wt_ref[...])
    _inproj_tok_epilogue(zt, bfc_ref[...], gv_ref[...], u_ref, va_ref, qb_ref, qc_ref, lfc_ref)
    zT = _dot_nt(wT_ref[...], xn)
    kbT_ref[...] = zT[0:W_B]
    vbT_ref[...] = zT[W_B:2 * W_B]
    kcT_ref[...] = zT[2 * W_B:2 * W_B + W_C]
    vcT_ref[...] = zT[2 * W_B + W_C:2 * W_B + 2 * W_C]
    o = 2 * W_B + 2 * W_C
    lfr_ref[...] = _log_sigmoid(zT[o:o + SUBLANE] + bfr_ref[...])


def _inproj_sample_kernel(x_ref, g_ref, wt_ref, wkv_ref, bfc_ref, gv_ref,
                          u_ref, va_ref, qb_ref, qc_ref, lfc_ref,
                          kb_ref, vb_ref, kc_ref, vc_ref):
    xn = _rms(x_ref[...], g_ref[...]).astype(BF)
    zt = _dot(xn, wt_ref[...])
    _inproj_tok_epilogue(zt, bfc_ref[...], gv_ref[...], u_ref, va_ref, qb_ref, qc_ref, lfc_ref)
    zk = _dot(xn, wkv_ref[...])
    kb_ref[...] = zk[:, 0:W_B]
    vb_ref[...] = zk[:, W_B:2 * W_B]
    kc_ref[...] = zk[:, 2 * W_B:2 * W_B + W_C]
    vc_ref[...] = zk[:, 2 * W_B + W_C:2 * W_B + 2 * W_C]


def _split_w_in(w_in_l, b_f_l):
    w = w_in_l
    wf = jnp.pad(w[:, _O_F:], ((0, 0), (0, LANE - N_C)))
    w_tok = jnp.concatenate([w[:, _O_U:_O_QB], w[:, _O_QB:_O_KB], w[:, _O_QC:_O_KC], wf], axis=1).astype(BF)
    w_kv = jnp.concatenate([w[:, _O_KB:_O_QC], w[:, _O_KC:_O_F]], axis=1)
    wfT = jnp.pad(w[:, _O_F:].T, ((0, SUBLANE - N_C), (0, 0)))
    w_kvT = jnp.concatenate([w_kv.T, wfT], axis=0).astype(BF)
    bf_col = jnp.pad(b_f_l, (0, LANE - N_C)).reshape(1, LANE)
    bf_row = jnp.pad(b_f_l, (0, SUBLANE - N_C)).reshape(SUBLANE, 1)
    return w_tok, w_kv.astype(BF), w_kvT, bf_col, bf_row


def _const_spec(shape):
    nd = len(shape)
    return pl.BlockSpec(shape, lambda *_: (0,) * nd, pipeline_mode=pl.Buffered(1))


def _inproj_prompt(x, g, w_tok, w_kvT, bf_col, bf_row, gv, *, tm):
    B, S, D = x.shape
    nt = w_tok.shape[1]
    row = lambda w: pl.BlockSpec((None, tm, w), lambda b, s: (b, s, 0))
    colT = lambda h: pl.BlockSpec((None, h, tm), lambda b, s: (b, 0, s))
    f = lambda *sh: jax.ShapeDtypeStruct(sh, F32)
    return pl.pallas_call(
        _inproj_prompt_kernel,
        grid=(B, S // tm),
        in_specs=[row(D), _const_spec((1, D)), _const_spec((D, nt)), _const_spec(w_kvT.shape),
                  _const_spec((1, LANE)), _const_spec((SUBLANE, 1)), _const_spec((1, W_A))],
        out_specs=[row(W_A), row(W_A), row(W_B), row(W_C), row(LANE),
                   colT(W_B), colT(W_B), colT(W_C), colT(W_C), colT(SUBLANE)],
        out_shape=[f(B, S, W_A), f(B, S, W_A), f(B, S, W_B), f(B, S, W_C), f(B, S, LANE),
                   f(B, W_B, S), f(B, W_B, S), f(B, W_C, S), f(B, W_C, S), f(B, SUBLANE, S)],
        compiler_params=_cparams("parallel", "parallel"),
        name="inproj_prompt",
    )(x, g.reshape(1, D), w_tok, w_kvT, bf_col, bf_row, gv.reshape(1, W_A))


def _inproj_sample(x, g, w_tok, w_kv, bf_col, gv, *, tm):
    N, D = x.shape
    nt = w_tok.shape[1]
    row = lambda w: pl.BlockSpec((tm, w), lambda i: (i, 0))
    f = lambda *sh: jax.ShapeDtypeStruct(sh, F32)
    return pl.pallas_call(
        _inproj_sample_kernel,
        grid=(N // tm,),
        in_specs=[row(D), _const_spec((1, D)), _const_spec((D, nt)), _const_spec(w_kv.shape),
                  _const_spec((1, LANE)), _const_spec((1, W_A))],
        out_specs=[row(W_A), row(W_A), row(W_B), row(W_C), row(LANE),
                   row(W_B), row(W_B), row(W_C), row(W_C)],
        out_shape=[f(N, W_A), f(N, W_A), f(N, W_B), f(N, W_C), f(N, LANE),
                   f(N, W_B), f(N, W_B), f(N, W_C), f(N, W_C)],
        compiler_params=_cparams("parallel"),
        name="inproj_sample",
    )(x, g.reshape(1, D), w_tok, w_kv, bf_col, gv.reshape(1, W_A))


def _merge_kernel(x_ref, u_ref, va_ref, b_ref, c_ref, wmix_ref, bmix_ref, gmix_ref, wo_ref, y_ref, *, tm):
    lane_grp = lax.broadcasted_iota(jnp.int32, (CHUNK, W_A), 1) // HEAD_DIM
    a_rows = []
    for r in range(tm // CHUNK):
        va = va_ref[r * CHUNK:(r + 1) * CHUNK, :]
        mixed = bmix_ref[...]
        for g in range(N_A):
            vg = jnp.where(lane_grp == g, va, 0.0).astype(BF)
            mixed = mixed + _dot(wmix_ref[g], vg)
        a_rows.append(u_ref[r * CHUNK:(r + 1) * CHUNK, :] * mixed)
    a_out = jnp.concatenate(a_rows, axis=0) if len(a_rows) > 1 else a_rows[0]
    gm = gmix_ref[...]
    cat = jnp.concatenate([_rms(a_out, gm[:, 0:W_A]).astype(BF),
                           _rms(b_ref[...], gm[:, W_A:W_A + W_B]).astype(BF),
                           _rms(c_ref[...], gm[:, W_A + W_B:]).astype(BF)], axis=1)
    y_ref[...] = x_ref[...] + _dot(cat, wo_ref[...])


def _mix_operands_prompt(w_s_l, b_s_l):
    wmix = jnp.tril(w_s_l).astype(BF)
    bmix = jnp.repeat(b_s_l.T, HEAD_DIM, axis=1)
    return wmix, bmix


def _mix_operands_sample(w_s_l, b_s_l, t):
    reps = CHUNK // t
    small = jnp.tril(w_s_l[:, :t, :t])
    wmix = jax.vmap(lambda m: jnp.kron(jnp.eye(reps, dtype=m.dtype), m))(small).astype(BF)
    bmix = jnp.tile(jnp.repeat(b_s_l[:, :t].T, HEAD_DIM, axis=1), (reps, 1))
    return wmix, bmix


def _merge(x, u, va, b_out, c_out, wmix, bmix, gmix, wo, *, tm):
    N, D = x.shape
    row = lambda w: pl.BlockSpec((tm, w), lambda i: (i, 0))
    return pl.pallas_call(
        functools.partial(_merge_kernel, tm=tm),
        grid=(N // tm,),
        in_specs=[row(D), row(W_A), row(W_A), row(W_B), row(W_C),
                  _const_spec((N_A, CHUNK, CHUNK)), _const_spec((CHUNK, W_A)),
                  _const_spec((1, D)), _const_spec((D, D))],
        out_specs=row(D),
        out_shape=jax.ShapeDtypeStruct((N, D), F32),
        compiler_params=_cparams("parallel"),
        name="merge",
    )(x, u, va, b_out, c_out, wmix, bmix, gmix.reshape(1, D), wo)


FF_CHUNK = 256


def _ffn_kernel(x_ref, g_ref, win_ref, wout_ref, gfin_ref, y_ref, *, final_norm):
    x = x_ref[...]
    xn = _rms(x, g_ref[...]).astype(BF)
    acc = x
    for c in range(D_FF // FF_CHUNK):
        h = _dot(xn, win_ref[:, c * FF_CHUNK:(c + 1) * FF_CHUNK])
        gate = _dot(xn, win_ref[:, D_FF + c * FF_CHUNK:D_FF + (c + 1) * FF_CHUNK])
        act = (h * jax.nn.sigmoid(h) * gate).astype(BF)
        acc = acc + _dot(act, wout_ref[c * FF_CHUNK:(c + 1) * FF_CHUNK, :])
    if final_norm:
        acc = _rms(acc, gfin_ref[...])
    y_ref[...] = acc


def _ffn(x, g, win, wout, gfin, *, tm, final_norm):
    N, D = x.shape
    row = pl.BlockSpec((tm, D), lambda i: (i, 0))
    return pl.pallas_call(
        functools.partial(_ffn_kernel, final_norm=final_norm),
        grid=(N // tm,),
        in_specs=[row, _const_spec((1, D)), _const_spec((D, 2 * D_FF)), _const_spec((D_FF, D)),
                  _const_spec((1, D))],
        out_specs=row,
        out_shape=jax.ShapeDtypeStruct((N, D), F32),
        compiler_params=_cparams("parallel"),
        name="ffn",
    )(x, g.reshape(1, D), win, wout, gfin.reshape(1, D))


def _split3(x):
    hi = x.astype(BF)
    r = x - hi.astype(F32)
    mid = r.astype(BF)
    lo = (r - mid.astype(F32)).astype(BF)
    return hi, mid, lo


def _fcum_kernel(lfr_ref, lfc_ref, fr_ref, fc_ref, *, S):
    ii = lax.broadcasted_iota(jnp.int32, (BLK, BLK), 0)
    jj = lax.broadcasted_iota(jnp.int32, (BLK, BLK), 1)
    t_incl = jnp.where(ii <= jj, 1.0, 0.0).astype(BF)
    l_incl = jnp.where(jj <= ii, 1.0, 0.0).astype(BF)
    rhs1 = jnp.concatenate([t_incl, jnp.ones((BLK, BLK), BF)], axis=1)
    rhs = jnp.concatenate([rhs1, rhs1, rhs1], axis=0)
    carry_r = jnp.zeros((SUBLANE, BLK), F32)
    carry_c = jnp.zeros((1, LANE), F32)
    for blk in range(S // BLK):
        sl = slice(blk * BLK, (blk + 1) * BLK)
        cs = _dot(jnp.concatenate(_split3(lfr_ref[:, sl]), axis=1), rhs)
        fr_ref[:, sl] = cs[:, :BLK] + carry_r
        carry_r = carry_r + cs[:, BLK:]
        hi, mid, lo = _split3(lfc_ref[sl, :])
        fc = _dot(l_incl, hi) + _dot(l_incl, mid) + _dot(l_incl, lo) + carry_c
        fc_ref[sl, :] = fc
        carry_c = fc[BLK - 1:BLK, :]


def _fcum(lf_row, lf_col):
    B, _, S = lf_row.shape
    return pl.pallas_call(
        functools.partial(_fcum_kernel, S=S),
        grid=(B,),
        in_specs=[pl.BlockSpec((None, SUBLANE, S), lambda b: (b, 0, 0)),
                  pl.BlockSpec((None, S, LANE), lambda b: (b, 0, 0))],
        out_specs=[pl.BlockSpec((None, SUBLANE, S), lambda b: (b, 0, 0)),
                   pl.BlockSpec((None, S, LANE), lambda b: (b, 0, 0))],
        out_shape=[jax.ShapeDtypeStruct((B, SUBLANE, S), F32), jax.ShapeDtypeStruct((B, S, LANE), F32)],
        compiler_params=_cparams("parallel"),
        name="fox_cumsum",
    )(lf_row, lf_col)


def _kblock(ref, h, j):
    return ref[h * HEAD_DIM:(h + 1) * HEAD_DIM, pl.ds(pl.multiple_of(j * BLK, BLK), BLK)]


def _fox_prompt_kernel(q_ref, kT_ref, vT_ref, fr_ref, fc_ref, o_ref, *, heads_per_group):
    qi = pl.program_id(1)
    row = lax.broadcasted_iota(jnp.int32, (BLK, BLK), 0)
    col = lax.broadcasted_iota(jnp.int32, (BLK, BLK), 1)
    causal = col <= row
    ones_rows = jnp.ones((SUBLANE, BLK), BF)
    outs = []
    for g0 in range(0, N_C, heads_per_group):
        heads = list(range(g0, g0 + heads_per_group))
        qs = [q_ref[:, h * HEAD_DIM:(h + 1) * HEAD_DIM].astype(BF) for h in heads]
        fcols = [fc_ref[:, h:h + 1] for h in heads]

        def step(j, carry, masked):
            new = []
            for (m, acc), h, qh, fcol in zip(carry, heads, qs, fcols):
                kT = _kblock(kT_ref, h, j).astype(BF)
                frow = fr_ref[h:h + 1, pl.ds(pl.multiple_of(j * BLK, BLK), BLK)]
                z = _dot(qh, kT) + fcol - frow
                if masked:
                    z = jnp.where(causal, z, NEG)
                m_new = jnp.maximum(m, jnp.max(z, axis=1, keepdims=True))
                p = jnp.exp(z - m_new).astype(BF)
                v_ext = jnp.concatenate([_kblock(vT_ref, h, j).astype(BF), ones_rows], axis=0)
                acc = acc * jnp.exp(m - m_new) + _dot_nt(p, v_ext)
                new.append((m_new, acc))
            return tuple(new)

        init = tuple((jnp.full((BLK, 1), NEG, F32), jnp.zeros((BLK, HEAD_DIM + SUBLANE), F32)) for _ in heads)
        carry = lax.fori_loop(0, qi, lambda j, c: step(j, c, False), init)
        carry = step(qi, carry, True)
        for m, acc in carry:
            outs.append(acc[:, :HEAD_DIM] / acc[:, HEAD_DIM:HEAD_DIM + 1])
    o_ref[...] = jnp.concatenate(outs, axis=1)


def _fox_prompt(q, kT, vT, f_row, f_col, *, heads_per_group=2):
    B, S, W = q.shape
    return pl.pallas_call(
        functools.partial(_fox_prompt_kernel, heads_per_group=heads_per_group),
        grid=(B, S // BLK),
        in_specs=[pl.BlockSpec((None, BLK, W), lambda b, i: (b, i, 0)),
                  pl.BlockSpec((None, W, S), lambda b, i: (b, 0, 0)),
                  pl.BlockSpec((None, W, S), lambda b, i: (b, 0, 0)),
                  pl.BlockSpec((None, SUBLANE, S), lambda b, i: (b, 0, 0)),
                  pl.BlockSpec((None, BLK, LANE), lambda b, i: (b, i, 0))],
        out_specs=pl.BlockSpec((None, BLK, W), lambda b, i: (b, i, 0)),
        out_shape=jax.ShapeDtypeStruct((B, S, W), F32),
        compiler_params=_cparams("parallel", "arbitrary"),
        name="fox_prompt",
    )(q, kT, vT, f_row, f_col)


def _suffix_sum_rhs():
    k = lax.broadcasted_iota(jnp.int32, (2 * BLK, 2 * BLK), 0) % BLK
    n = lax.broadcasted_iota(jnp.int32, (2 * BLK, 2 * BLK), 1)
    return jnp.where((n >= BLK) | (k > n), 1.0, 0.0).astype(BF)


def _sb_block(z, c, rhs, mask):
    sp = jnp.log(1.0 + jnp.exp(-jnp.abs(z)))
    ls = jnp.minimum(z, 0.0) - sp
    l1m = ls - z
    if mask is not None:
        l1m = jnp.where(mask, l1m, 0.0)
    hi = l1m.astype(BF)
    lo = (l1m - hi.astype(F32)).astype(BF)
    cs = _dot(jnp.concatenate([hi, lo], axis=1), rhs)
    n = z.shape[1]
    a = jnp.exp(ls + cs[:, :n] + c)
    if mask is not None:
        a = jnp.where(mask, a, 0.0)
    return a, c + cs[:, n:]


def _sb_prompt_kernel(q_ref, kT_ref, vT_ref, o_ref, *, heads_per_group):
    qi = pl.program_id(1)
    row = lax.broadcasted_iota(jnp.int32, (BLK, BLK), 0)
    col = lax.broadcasted_iota(jnp.int32, (BLK, BLK), 1)
    strict = col < row
    rhs = _suffix_sum_rhs()
    outs = []
    for g0 in range(0, N_B, heads_per_group):
        heads = list(range(g0, g0 + heads_per_group))
        qs = [q_ref[:, h * HEAD_DIM:(h + 1) * HEAD_DIM].astype(BF) for h in heads]

        def step(j, carry, mask):
            new = []
            for (c, acc), h, qh in zip(carry, heads, qs):
                z = _dot(qh, _kblock(kT_ref, h, j).astype(BF))
                a, c = _sb_block(z, c, rhs, mask)
                acc = acc + _dot_nt(a.astype(BF), _kblock(vT_ref, h, j).astype(BF))
                new.append((c, acc))
            return tuple(new)

        init = tuple((jnp.zeros((BLK, BLK), F32), jnp.zeros((BLK, HEAD_DIM), F32)) for _ in heads)
        carry = step(qi, init, strict)
        carry = lax.fori_loop(0, qi, lambda i, cr: step(qi - 1 - i, cr, None), carry)
        for _, acc in carry:
            outs.append(acc)
    o_ref[...] = jnp.concatenate(outs, axis=1)


def _sb_prompt(q, kT, vT, *, heads_per_group=2):
    B, S, W = q.shape
    return pl.pallas_call(
        functools.partial(_sb_prompt_kernel, heads_per_group=heads_per_group),
        grid=(B, S // BLK),
        in_specs=[pl.BlockSpec((None, BLK, W), lambda b, i: (b, i, 0)),
                  pl.BlockSpec((None, W, S), lambda b, i: (b, 0, 0)),
                  pl.BlockSpec((None, W, S), lambda b, i: (b, 0, 0))],
        out_specs=pl.BlockSpec((None, BLK, W), lambda b, i: (b, i, 0)),
        out_shape=jax.ShapeDtypeStruct((B, S, W), F32),
        compiler_params=_cparams("parallel", "arbitrary"),
        name="sb_prompt",
    )(q, kT, vT)


def _stack_heads(parts):
    return jnp.concatenate(parts, axis=0)


def _decode_kernel(pt_ref, qb_ref, qc_ref, kb_ref, vb_ref, kc_ref, vc_ref, lf_ref, *refs, n_pages, t_new):
    del pt_ref
    n = n_pages
    sbk, sbv, fk, fv, flf = (refs[i * n:(i + 1) * n] for i in range(5))
    ob_ref, oc_ref = refs[5 * n], refs[5 * n + 1]
    T = t_new
    rows = N_B * T
    hs = lambda x, h: x[:, h * HEAD_DIM:(h + 1) * HEAD_DIM]

    sb_rhs = _suffix_sum_rhs()
    kk = lax.broadcasted_iota(jnp.int32, (3 * PAGE, 2 * PAGE), 0) % PAGE
    nn = lax.broadcasted_iota(jnp.int32, (3 * PAGE, 2 * PAGE), 1)
    lf_rhs = jnp.where((nn >= PAGE) | (kk > nn), 1.0, 0.0).astype(BF)
    t_idx = lax.broadcasted_iota(jnp.int32, (rows, PAGE), 0) % T
    s_idx = lax.broadcasted_iota(jnp.int32, (rows, PAGE), 1)
    zpad = jnp.zeros((PAGE - T, HEAD_DIM), BF)
    ones_rows = jnp.ones((SUBLANE, PAGE), BF)

    qb = qb_ref[...].astype(BF)
    kb_new = kb_ref[...].astype(BF)
    vb_new = vb_ref[...].astype(BF)
    z = _stack_heads([_dot_nt(hs(qb, h), jnp.concatenate([hs(kb_new, h), zpad], axis=0)) for h in range(N_B)])
    a, c = _sb_block(z, jnp.zeros((rows, PAGE), F32), sb_rhs, s_idx < t_idx)
    a = a.astype(BF)
    acc = [_dot(a[h * T:(h + 1) * T], jnp.concatenate([hs(vb_new, h), zpad], axis=0)) for h in range(N_B)]
    for p in reversed(range(n)):
        z = _stack_heads([_dot(hs(qb, h), sbk[p][h].astype(BF)) for h in range(N_B)])
        a, c = _sb_block(z, c, sb_rhs, None)
        a = a.astype(BF)
        acc = [acc[h] + _dot_nt(a[h * T:(h + 1) * T], sbv[p][h].astype(BF)) for h in range(N_B)]
    ob_ref[...] = jnp.concatenate(acc, axis=1)

    lf_new = lf_ref[...]
    g_rows = [lf_new[0:1]]
    for t in range(1, T):
        g_rows.append(g_rows[-1] + lf_new[t:t + 1])
    g_col = jnp.concatenate(g_rows, axis=0)
    eye = lax.broadcasted_iota(jnp.int32, (T, PAGE), 0) == lax.broadcasted_iota(jnp.int32, (T, PAGE), 1)
    g_q = _stack_heads([jnp.broadcast_to(g_col[:, h:h + 1], (T, PAGE)) for h in range(N_C)])
    g_k = _stack_heads([jnp.broadcast_to(
        jnp.sum(jnp.where(eye, jnp.broadcast_to(g_col[:, h:h + 1], (T, PAGE)), 0.0), axis=0, keepdims=True),
        (T, PAGE)) for h in range(N_C)])
    qc = qc_ref[...].astype(BF)
    kc_new = kc_ref[...].astype(BF)
    vc_new = vc_ref[...].astype(BF)

    def fox_update(z, m, den, accs, pv_fn):
        m_new = jnp.maximum(m, jnp.max(z, axis=1, keepdims=True))
        pmat = jnp.exp(z - m_new).astype(BF)
        alpha = jnp.exp(m - m_new)
        pv = _stack_heads([pv_fn(pmat[h * T:(h + 1) * T], h) for h in range(N_C)])
        ps = _stack_heads([_dot_nt(pmat[h * T:(h + 1) * T], ones_rows) for h in range(N_C)])
        return m_new, den * alpha + ps, accs * alpha + pv

    z = _stack_heads([_dot_nt(hs(qc, h), jnp.concatenate([hs(kc_new, h), zpad], axis=0)) for h in range(N_C)])
    z = jnp.where(s_idx <= t_idx, z + g_q - g_k, NEG)
    m, den, accs = fox_update(
        z, jnp.full((rows, 1), NEG, F32), jnp.zeros((rows, SUBLANE), F32), jnp.zeros((rows, HEAD_DIM), F32),
        lambda ph, h: _dot(ph, jnp.concatenate([hs(vc_new, h), zpad], axis=0)))
    r_carry = jnp.zeros((SUBLANE, PAGE), F32)
    lf_pad = jnp.zeros((SUBLANE - N_C, PAGE), F32)
    for p in reversed(range(n)):
        lf_page = jnp.concatenate([flf[p][...], lf_pad], axis=0)
        cs = _dot(jnp.concatenate(_split3(lf_page), axis=1), lf_rhs)
        r_page = cs[:, :PAGE] + r_carry
        r_carry = r_carry + cs[:, PAGE:]
        bias = _stack_heads([jnp.broadcast_to(r_page[h:h + 1], (T, PAGE)) for h in range(N_C)]) + g_q
        z = _stack_heads([_dot(hs(qc, h), fk[p][h].astype(BF)) for h in range(N_C)]) + bias
        m, den, accs = fox_update(z, m, den, accs, lambda ph, h: _dot_nt(ph, fv[p][h].astype(BF)))
    out = accs / den[:, 0:1]
    oc_ref[...] = jnp.concatenate([out[h * T:(h + 1) * T] for h in range(N_C)], axis=1)


def _decode(page_table, qb, qc, kb, vb, kc, vc, lf, sbkT, sbvT, fkT, fvT, flfT, *, layer, t_new):
    nb, n_pages = page_table.shape
    N = qb.shape[0]
    tok = lambda w: pl.BlockSpec((t_new, w), lambda b, pt: (b, 0))
    kv_specs = [pl.BlockSpec((None, None, N_B, HEAD_DIM, PAGE), functools.partial(
        lambda b, pt, p: (layer, pt[b, p], 0, 0, 0), p=p)) for p in range(n_pages)]
    lf_specs = [pl.BlockSpec((None, None, N_C, PAGE), functools.partial(
        lambda b, pt, p: (layer, pt[b, p], 0, 0), p=p)) for p in range(n_pages)]
    return pl.pallas_call(
        functools.partial(_decode_kernel, n_pages=n_pages, t_new=t_new),
        grid_spec=pltpu.PrefetchScalarGridSpec(
            num_scalar_prefetch=1,
            grid=(nb,),
            in_specs=[tok(W_B), tok(W_C), tok(W_B), tok(W_B), tok(W_C), tok(W_C), tok(LANE)]
                     + kv_specs * 4 + lf_specs,
            out_specs=[tok(W_B), tok(W_C)],
        ),
        out_shape=[jax.ShapeDtypeStruct((N, W_B), F32), jax.ShapeDtypeStruct((N, W_C), F32)],
        compiler_params=_cparams("parallel"),
        name="decode_attention",
    )(page_table, qb, qc, kb, vb, kc, vc, lf,
      *([sbkT] * n_pages), *([sbvT] * n_pages), *([fkT] * n_pages), *([fvT] * n_pages), *([flfT] * n_pages))


TM_PROMPT = 512
TM_SAMPLE = 256


def kernel(x_prompt, x_sample, cache_sb_k, cache_sb_v, cache_fox_k, cache_fox_v, cache_fox_logf, page_table, g_attn, w_in, b_f, g_v, w_s, b_s, g_mix, w_o, g_ffn, w_ffn_in, w_ffn_out, g_final):
    B, S, D = x_prompt.shape
    DB, T, _ = x_sample.shape
    depth = g_attn.shape[0]
    assert D == D_MODEL and S % TM_PROMPT == 0 and (DB * T) % TM_SAMPLE == 0
    assert CHUNK % T == 0 and TM_SAMPLE % CHUNK == 0 and cache_sb_k.shape[2] == PAGE

    kv_t = lambda c: jnp.transpose(c, (0, 1, 3, 4, 2))
    sbkT, sbvT, fkT, fvT = kv_t(cache_sb_k), kv_t(cache_sb_v), kv_t(cache_fox_k), kv_t(cache_fox_v)
    flfT = jnp.transpose(cache_fox_logf, (0, 1, 3, 2))

    yp = x_prompt
    ys = x_sample.reshape(DB * T, D)
    outs_p = [[] for _ in range(5)]
    outs_s = [[] for _ in range(6)]
    for l in range(depth):
        last = l == depth - 1
        w_tok, w_kv, w_kvT, bf_col, bf_row = _split_w_in(w_in[l], b_f[l])
        wo, wfi, wfo = w_o[l].astype(BF), w_ffn_in[l].astype(BF), w_ffn_out[l].astype(BF)

        u, va, qb, qc, lfc, kbT, vbT, kcT, vcT, lfr = _inproj_prompt(
            yp, g_attn[l], w_tok, w_kvT, bf_col, bf_row, g_v[l], tm=TM_PROMPT)
        f_row, f_col = _fcum(lfr, lfc)
        c_out = _fox_prompt(qc, kcT, vcT, f_row, f_col)
        b_out = _sb_prompt(qb, kbT, vbT)
        wmix, bmix = _mix_operands_prompt(w_s[l], b_s[l])
        flat = lambda a: a.reshape(B * S, a.shape[-1])
        y1 = _merge(flat(yp), flat(u), flat(va), flat(b_out), flat(c_out), wmix, bmix, g_mix[l], wo, tm=TM_PROMPT)
        yp = _ffn(y1, g_ffn[l], wfi, wfo, g_final, tm=TM_PROMPT, final_norm=last).reshape(B, S, D)
        for dst, a in zip(outs_p, (kbT, vbT, kcT, vcT)):
            dst.append(a.reshape(B, N_B, HEAD_DIM, S))
        outs_p[4].append(lfr[:, :N_C, :])

        u, va, qb, qc, lfc, kb, vb, kc, vc = _inproj_sample(
            ys, g_attn[l], w_tok, w_kv, bf_col, g_v[l], tm=TM_SAMPLE)
        b_out, c_out = _decode(page_table, qb, qc, kb, vb, kc, vc, lfc, sbkT, sbvT, fkT, fvT, flfT,
                               layer=l, t_new=T)
        wmix, bmix = _mix_operands_sample(w_s[l], b_s[l], T)
        y1 = _merge(ys, u, va, b_out, c_out, wmix, bmix, g_mix[l], wo, tm=TM_SAMPLE)
        ys = _ffn(y1, g_ffn[l], wfi, wfo, g_final, tm=TM_SAMPLE, final_norm=last)
        for dst, a in zip(outs_s, (kb, vb, kc, vc)):
            dst.append(a.reshape(DB, T, N_B, HEAD_DIM))
        outs_s[4].append(lfc[:, :N_C].reshape(DB, T, N_C))
        outs_s[5].append(va.reshape(DB, T, W_A))

    to_seq_major = lambda xs: jnp.transpose(jnp.stack(xs), (0, 1, 4, 2, 3))
    p_lf = jnp.transpose(jnp.stack(outs_p[4]), (0, 1, 3, 2))
    return (yp, ys.reshape(DB, T, D),
            to_seq_major(outs_p[0]), to_seq_major(outs_p[1]), to_seq_major(outs_p[2]), to_seq_major(outs_p[3]), p_lf,
            jnp.stack(outs_s[0]), jnp.stack(outs_s[1]), jnp.stack(outs_s[2]), jnp.stack(outs_s[3]),
            jnp.stack(outs_s[4]), jnp.stack(outs_s[5]))
```

```python
import functools
import math

import jax
import jax.numpy as jnp
import numpy as np
from jax import lax
from jax.experimental import pallas as pl
from jax.experimental.pallas import tpu as pltpu

D_MODEL = 1024
HEAD_DIM = 64
N_A, N_B, N_C = 4, 6, 6
W_A, W_B, W_C = N_A * HEAD_DIM, N_B * HEAD_DIM, N_C * HEAD_DIM
CHUNK = 128
PAGE = 128
D_FF = 2816
EPS = 1e-6
NEG = -1e30
LANE = 128
SUBLANE = 8
BLK = 128
VMEM_LIMIT = 48 * 1024 * 1024
BF = jnp.bfloat16
F32 = jnp.float32

_O_U, _O_VA, _O_QB, _O_KB, _O_VB = 0, W_A, 2 * W_A, 2 * W_A + W_B, 2 * W_A + 2 * W_B
_O_QC = 2 * W_A + 3 * W_B
_O_KC, _O_VC, _O_F = _O_QC + W_C, _O_QC + 2 * W_C, _O_QC + 3 * W_C


def _cparams(*sem):
    return pltpu.CompilerParams(dimension_semantics=sem, vmem_limit_bytes=VMEM_LIMIT)


def _gelu(x):
    return 0.5 * x * (1.0 + jnp.tanh(math.sqrt(2.0 / math.pi) * (x + 0.044715 * (x * x * x))))


def _log_sigmoid(x):
    return jnp.minimum(x, 0.0) - jnp.log(1.0 + jnp.exp(-jnp.abs(x)))


def _rms(x, g):
    return x * lax.rsqrt(jnp.mean(x * x, axis=-1, keepdims=True) + EPS) * g


def _dot(a, b):
    return jnp.dot(a, b, preferred_element_type=F32)


def _dot_nt(a, b):
    return lax.dot_general(a, b, (((1,), (1,)), ((), ())), preferred_element_type=F32)


def _inproj_tok_epilogue(zt, bf_col, gv, u_ref, va_ref, qb_ref, qc_ref, lfc_ref):
    u_ref[...] = _gelu(zt[:, 0:W_A])
    va_ref[...] = _rms(_gelu(zt[:, W_A:2 * W_A]), gv)
    qb_ref[...] = zt[:, 2 * W_A:2 * W_A + W_B] * (HEAD_DIM ** -0.5)
    qc_ref[...] = zt[:, 2 * W_A + W_B:2 * W_A + W_B + W_C] * (HEAD_DIM ** -0.5)
    o = 2 * W_A + W_B + W_C
    lfc_ref[...] = _log_sigmoid(zt[:, o:o + LANE] + bf_col)


def _inproj_prompt_kernel(x_ref, g_ref, wt_ref, wT_ref, bfc_ref, bfr_ref, gv_ref,
                          u_ref, va_ref, qb_ref, qc_ref, lfc_ref,
                          kbT_ref, vbT_ref, kcT_ref, vcT_ref, lfr_ref):
    xn = _rms(x_ref[...], g_ref[...]).astype(BF)
    zt = _dot(xn, wt_ref[...])
    _inproj_tok_epilogue(zt, bfc_ref[...], gv_ref[...], u_ref, va_ref, qb_ref, qc_ref, lfc_ref)
    zT = _dot_nt(wT_ref[...], xn)
    kbT_ref[...] = zT[0:W_B]
    vbT_ref[...] = zT[W_B:2 * W_B]
    kcT_ref[...] = zT[2 * W_B:2 * W_B + W_C]
    vcT_ref[...] = zT[2 * W_B + W_C:2 * W_B + 2 * W_C]
    o = 2 * W_B + 2 * W_C
    lfr_ref[...] = _log_sigmoid(zT[o:o + SUBLANE] + bfr_ref[...])


def _inproj_sample_kernel(x_ref, g_ref, wt_ref, wkv_ref, bfc_ref, gv_ref,
                          u_ref, va_ref, qb_ref, qc_ref, lfc_ref,
                          kb_ref, vb_ref, kc_ref, vc_ref):
    xn = _rms(x_ref[...], g_ref[...]).astype(BF)
    zt = _dot(xn, wt_ref[...])
    _inproj_tok_epilogue(zt, bfc_ref[...], gv_ref[...], u_ref, va_ref, qb_ref, qc_ref, lfc_ref)
    zk = _dot(xn, wkv_ref[...])
    kb_ref[...] = zk[:, 0:W_B]
    vb_ref[...] = zk[:, W_B:2 * W_B]
    kc_ref[...] = zk[:, 2 * W_B:2 * W_B + W_C]
    vc_ref[...] = zk[:, 2 * W_B + W_C:2 * W_B + 2 * W_C]


def _split_w_in(w_in_l, b_f_l):
    w = w_in_l
    wf = jnp.pad(w[:, _O_F:], ((0, 0), (0, LANE - N_C)))
    w_tok = jnp.concatenate([w[:, _O_U:_O_QB], w[:, _O_QB:_O_KB], w[:, _O_QC:_O_KC], wf], axis=1).astype(BF)
    w_kv = jnp.concatenate([w[:, _O_KB:_O_QC], w[:, _O_KC:_O_F]], axis=1)
    wfT = jnp.pad(w[:, _O_F:].T, ((0, SUBLANE - N_C), (0, 0)))
    w_kvT = jnp.concatenate([w_kv.T, wfT], axis=0).astype(BF)
    bf_col = jnp.pad(b_f_l, (0, LANE - N_C)).reshape(1, LANE)
    bf_row = jnp.pad(b_f_l, (0, SUBLANE - N_C)).reshape(SUBLANE, 1)
    return w_tok, w_kv.astype(BF), w_kvT, bf_col, bf_row


def _const_spec(shape):
    nd = len(shape)
    return pl.BlockSpec(shape, lambda *_: (0,) * nd, pipeline_mode=pl.Buffered(1))


def _inproj_prompt(x, g, w_tok, w_kvT, bf_col, bf_row, gv, *, tm):
    B, S, D = x.shape
    nt = w_tok.shape[1]
    row = lambda w: pl.BlockSpec((None, tm, w), lambda b, s: (b, s, 0))
    colT = lambda h: pl.BlockSpec((None, h, tm), lambda b, s: (b, 0, s))
    f = lambda *sh: jax.ShapeDtypeStruct(sh, F32)
    return pl.pallas_call(
        _inproj_prompt_kernel,
        grid=(B, S // tm),
        in_specs=[row(D), _const_spec((1, D)), _const_spec((D, nt)), _const_spec(w_kvT.shape),
                  _const_spec((1, LANE)), _const_spec((SUBLANE, 1)), _const_spec((1, W_A))],
        out_specs=[row(W_A), row(W_A), row(W_B), row(W_C), row(LANE),
                   colT(W_B), colT(W_B), colT(W_C), colT(W_C), colT(SUBLANE)],
        out_shape=[f(B, S, W_A), f(B, S, W_A), f(B, S, W_B), f(B, S, W_C), f(B, S, LANE),
                   f(B, W_B, S), f(B, W_B, S), f(B, W_C, S), f(B, W_C, S), f(B, SUBLANE, S)],
        compiler_params=_cparams("parallel", "parallel"),
        name="inproj_prompt",
    )(x, g.reshape(1, D), w_tok, w_kvT, bf_col, bf_row, gv.reshape(1, W_A))


def _inproj_sample(x, g, w_tok, w_kv, bf_col, gv, *, tm):
    N, D = x.shape
    nt = w_tok.shape[1]
    row = lambda w: pl.BlockSpec((tm, w), lambda i: (i, 0))
    f = lambda *sh: jax.ShapeDtypeStruct(sh, F32)
    return pl.pallas_call(
        _inproj_sample_kernel,
        grid=(N // tm,),
        in_specs=[row(D), _const_spec((1, D)), _const_spec((D, nt)), _const_spec(w_kv.shape),
                  _const_spec((1, LANE)), _const_spec((1, W_A))],
        out_specs=[row(W_A), row(W_A), row(W_B), row(W_C), row(LANE),
                   row(W_B), row(W_B), row(W_C), row(W_C)],
        out_shape=[f(N, W_A), f(N, W_A), f(N, W_B), f(N, W_C), f(N, LANE),
                   f(N, W_B), f(N, W_B), f(N, W_C), f(N, W_C)],
        compiler_params=_cparams("parallel"),
        name="inproj_sample",
    )(x, g.reshape(1, D), w_tok, w_kv, bf_col, gv.reshape(1, W_A))


def _merge_kernel(x_ref, u_ref, va_ref, b_ref, c_ref, wmix_ref, bmix_ref, gmix_ref, wo_ref, y_ref, *, tm):
    lane_grp = lax.broadcasted_iota(jnp.int32, (CHUNK, W_A), 1) // HEAD_DIM
    a_rows = []
    for r in range(tm // CHUNK):
        va = va_ref[r * CHUNK:(r + 1) * CHUNK, :]
        mixed = bmix_ref[...]
        for g in range(N_A):
            vg = jnp.where(lane_grp == g, va, 0.0).astype(BF)
            mixed = mixed + _dot(wmix_ref[g], vg)
        a_rows.append(u_ref[r * CHUNK:(r + 1) * CHUNK, :] * mixed)
    a_out = jnp.concatenate(a_rows, axis=0) if len(a_rows) > 1 else a_rows[0]
    gm = gmix_ref[...]
    cat = jnp.concatenate([_rms(a_out, gm[:, 0:W_A]).astype(BF),
                           _rms(b_ref[...], gm[:, W_A:W_A + W_B]).astype(BF),
                           _rms(c_ref[...], gm[:, W_A + W_B:]).astype(BF)], axis=1)
    y_ref[...] = x_ref[...] + _dot(cat, wo_ref[...])


def _mix_operands_prompt(w_s_l, b_s_l):
    wmix = jnp.tril(w_s_l).astype(BF)
    bmix = jnp.repeat(b_s_l.T, HEAD_DIM, axis=1)
    return wmix, bmix


def _mix_operands_sample(w_s_l, b_s_l, t):
    reps = CHUNK // t
    small = jnp.tril(w_s_l[:, :t, :t])
    wmix = jax.vmap(lambda m: jnp.kron(jnp.eye(reps, dtype=m.dtype), m))(small).astype(BF)
    bmix = jnp.tile(jnp.repeat(b_s_l[:, :t].T, HEAD_DIM, axis=1), (reps, 1))
    return wmix, bmix


def _merge(x, u, va, b_out, c_out, wmix, bmix, gmix, wo, *, tm):
    N, D = x.shape
    row = lambda w: pl.BlockSpec((tm, w), lambda i: (i, 0))
    return pl.pallas_call(
        functools.partial(_merge_kernel, tm=tm),
        grid=(N // tm,),
        in_specs=[row(D), row(W_A), row(W_A), row(W_B), row(W_C),
                  _const_spec((N_A, CHUNK, CHUNK)), _const_spec((CHUNK, W_A)),
                  _const_spec((1, D)), _const_spec((D, D))],
        out_specs=row(D),
        out_shape=jax.ShapeDtypeStruct((N, D), F32),
        compiler_params=_cparams("parallel"),
        name="merge",
    )(x, u, va, b_out, c_out, wmix, bmix, gmix.reshape(1, D), wo)


FF_CHUNK = 256


def _ffn_kernel(x_ref, g_ref, win_ref, wout_ref, gfin_ref, y_ref, *, final_norm):
    x = x_ref[...]
    xn = _rms(x, g_ref[...]).astype(BF)
    acc = x
    for c in range(D_FF // FF_CHUNK):
        h = _dot(xn, win_ref[:, c * FF_CHUNK:(c + 1) * FF_CHUNK])
        gate = _dot(xn, win_ref[:, D_FF + c * FF_CHUNK:D_FF + (c + 1) * FF_CHUNK])
        act = (h * jax.nn.sigmoid(h) * gate).astype(BF)
        acc = acc + _dot(act, wout_ref[c * FF_CHUNK:(c + 1) * FF_CHUNK, :])
    if final_norm:
        acc = _rms(acc, gfin_ref[...])
    y_ref[...] = acc


def _ffn(x, g, win, wout, gfin, *, tm, final_norm):
    N, D = x.shape
    row = pl.BlockSpec((tm, D), lambda i: (i, 0))
    return pl.pallas_call(
        functools.partial(_ffn_kernel, final_norm=final_norm),
        grid=(N // tm,),
        in_specs=[row, _const_spec((1, D)), _const_spec((D, 2 * D_FF)), _const_spec((D_FF, D)),
                  _const_spec((1, D))],
        out_specs=row,
        out_shape=jax.ShapeDtypeStruct((N, D), F32),
        compiler_params=_cparams("parallel"),
        name="ffn",
    )(x, g.reshape(1, D), win, wout, gfin.reshape(1, D))


def _split3(x):
    hi = x.astype(BF)
    r = x - hi.astype(F32)
    mid = r.astype(BF)
    lo = (r - mid.astype(F32)).astype(BF)
    return hi, mid, lo


def _fcum_kernel(lfr_ref, lfc_ref, fr_ref, fc_ref, *, S):
    ii = lax.broadcasted_iota(jnp.int32, (BLK, BLK), 0)
    jj = lax.broadcasted_iota(jnp.int32, (BLK, BLK), 1)
    t_incl = jnp.where(ii <= jj, 1.0, 0.0).astype(BF)
    l_incl = jnp.where(jj <= ii, 1.0, 0.0).astype(BF)
    rhs1 = jnp.concatenate([t_incl, jnp.ones((BLK, BLK), BF)], axis=1)
    rhs = jnp.concatenate([rhs1, rhs1, rhs1], axis=0)
    carry_r = jnp.zeros((SUBLANE, BLK), F32)
    carry_c = jnp.zeros((1, LANE), F32)
    for blk in range(S // BLK):
        sl = slice(blk * BLK, (blk + 1) * BLK)
        cs = _dot(jnp.concatenate(_split3(lfr_ref[:, sl]), axis=1), rhs)
        fr_ref[:, sl] = cs[:, :BLK] + carry_r
        carry_r = carry_r + cs[:, BLK:]
        hi, mid, lo = _split3(lfc_ref[sl, :])
        fc = _dot(l_incl, hi) + _dot(l_incl, mid) + _dot(l_incl, lo) + carry_c
        fc_ref[sl, :] = fc
        carry_c = fc[BLK - 1:BLK, :]


def _fcum(lf_row, lf_col):
    B, _, S = lf_row.shape
    return pl.pallas_call(
        functools.partial(_fcum_kernel, S=S),
        grid=(B,),
        in_specs=[pl.BlockSpec((None, SUBLANE, S), lambda b: (b, 0, 0)),
                  pl.BlockSpec((None, S, LANE), lambda b: (b, 0, 0))],
        out_specs=[pl.BlockSpec((None, SUBLANE, S), lambda b: (b, 0, 0)),
                   pl.BlockSpec((None, S, LANE), lambda b: (b, 0, 0))],
        out_shape=[jax.ShapeDtypeStruct((B, SUBLANE, S), F32), jax.ShapeDtypeStruct((B, S, LANE), F32)],
        compiler_params=_cparams("parallel"),
        name="fox_cumsum",
    )(lf_row, lf_col)


K_TILE_BLOCKS = 4


def _ktile(ref, h, g, kw):
    return ref[h * HEAD_DIM:(h + 1) * HEAD_DIM, pl.ds(pl.multiple_of(g * kw, kw), kw)]


def _fox_prompt_kernel(q_ref, kT_ref, vT_ref, fr_ref, fc_ref, o_ref, *, heads_per_group, kw):
    qi = pl.program_id(1)
    gd = (qi * BLK) // kw
    q_pos = qi * BLK + lax.broadcasted_iota(jnp.int32, (BLK, kw), 0)
    k_off = lax.broadcasted_iota(jnp.int32, (BLK, kw), 1)
    ones_rows = jnp.ones((SUBLANE, kw), BF)
    outs = []
    for g0 in range(0, N_C, heads_per_group):
        heads = list(range(g0, g0 + heads_per_group))
        qs = [q_ref[:, h * HEAD_DIM:(h + 1) * HEAD_DIM].astype(BF) for h in heads]
        fcols = [fc_ref[:, h:h + 1] for h in heads]

        def step(g, carry, masked):
            st = [dict(m=m, acc=acc) for m, acc in carry]
            for i, s in _issue_order(len(heads), 2, FOX_SKEW):
                h, d = heads[i], st[i]
                if s == 0:
                    frow = fr_ref[h:h + 1, pl.ds(pl.multiple_of(g * kw, kw), kw)]
                    z = _dot(qs[i], _ktile(kT_ref, h, g, kw).astype(BF)) + fcols[i] - frow
                    if masked:
                        z = jnp.where(g * kw + k_off <= q_pos, z, NEG)
                    m_new = jnp.maximum(d["m"], jnp.max(z, axis=1, keepdims=True))
                    d["p"] = jnp.exp(z - m_new).astype(BF)
                    d["alpha"] = jnp.exp(d["m"] - m_new)
                    d["m"] = m_new
                else:
                    v_ext = jnp.concatenate([_ktile(vT_ref, h, g, kw).astype(BF), ones_rows], axis=0)
                    d["acc"] = d["acc"] * d["alpha"] + _dot_nt(d["p"], v_ext)
            return tuple((d["m"], d["acc"]) for d in st)

        init = tuple((jnp.full((BLK, 1), NEG, F32), jnp.zeros((BLK, HEAD_DIM + SUBLANE), F32)) for _ in heads)
        carry = lax.fori_loop(0, gd, lambda g, c: step(g, c, False), init)
        carry = step(gd, carry, True)
        for m, acc in carry:
            outs.append(acc[:, :HEAD_DIM] / acc[:, HEAD_DIM:HEAD_DIM + 1])
    o_ref[...] = jnp.concatenate(outs, axis=1)


def _fox_prompt(q, kT, vT, f_row, f_col, *, heads_per_group=N_C):
    B, S, W = q.shape
    kw = min(K_TILE_BLOCKS * BLK, S)
    return pl.pallas_call(
        functools.partial(_fox_prompt_kernel, heads_per_group=heads_per_group, kw=kw),
        grid=(B, S // BLK),
        in_specs=[pl.BlockSpec((None, BLK, W), lambda b, i: (b, i, 0)),
                  pl.BlockSpec((None, W, S), lambda b, i: (b, 0, 0)),
                  pl.BlockSpec((None, W, S), lambda b, i: (b, 0, 0)),
                  pl.BlockSpec((None, SUBLANE, S), lambda b, i: (b, 0, 0)),
                  pl.BlockSpec((None, BLK, LANE), lambda b, i: (b, i, 0))],
        out_specs=pl.BlockSpec((None, BLK, W), lambda b, i: (b, i, 0)),
        out_shape=jax.ShapeDtypeStruct((B, S, W), F32),
        compiler_params=_cparams("parallel", "arbitrary"),
        name="fox_prompt",
    )(q, kT, vT, f_row, f_col)


def _suffix_sum_rhs():
    k = lax.broadcasted_iota(jnp.int32, (2 * BLK, 2 * BLK), 0) % BLK
    n = lax.broadcasted_iota(jnp.int32, (2 * BLK, 2 * BLK), 1)
    return jnp.where((n >= BLK) | (k > n), 1.0, 0.0).astype(BF)


def _sb_pre(z, mask):
    nb = z.shape[1] // BLK
    sp = jnp.log(1.0 + jnp.exp(-jnp.abs(z)))
    ls = jnp.minimum(z, 0.0) - sp
    l1m = ls - z
    if mask is not None:
        l1m = jnp.where(mask, l1m, 0.0)
    hi = l1m.astype(BF)
    lo = (l1m - hi.astype(F32)).astype(BF)
    blk = lambda x, b: x[:, b * BLK:(b + 1) * BLK]
    lhs = [jnp.concatenate([blk(hi, b), blk(lo, b)], axis=1) for b in range(nb)]
    return ls, (jnp.concatenate(lhs, axis=0) if nb > 1 else lhs[0])


def _sb_post(ls, cs, c, mask):
    r, n = ls.shape
    nb = n // BLK
    a = [None] * nb
    for b in reversed(range(nb)):
        csb = cs[b * r:(b + 1) * r]
        a[b] = jnp.exp(ls[:, b * BLK:(b + 1) * BLK] + csb[:, :BLK] + c)
        c = c + csb[:, BLK:]
    a = jnp.concatenate(a, axis=1) if nb > 1 else a[0]
    if mask is not None:
        a = jnp.where(mask, a, 0.0)
    return a, c


def _sb_block(z, c, rhs, mask):
    ls, lhs = _sb_pre(z, mask)
    return _sb_post(ls, _dot(lhs, rhs), c, mask)


def _issue_order(n_chains, n_stages, dist):
    items = [(h + s * dist, s, h) for h in range(n_chains) for s in range(n_stages)]
    return [(h, s) for _, s, h in sorted(items)]


SB_SKEW = 2
FOX_SKEW = 2


def _sb_prompt_kernel(q_ref, kT_ref, vT_ref, o_ref, *, heads_per_group, kw):
    qi = pl.program_id(1)
    gd = (qi * BLK) // kw
    q_pos = qi * BLK + lax.broadcasted_iota(jnp.int32, (BLK, kw), 0)
    k_off = lax.broadcasted_iota(jnp.int32, (BLK, kw), 1)
    rhs = _suffix_sum_rhs()
    outs = []
    for g0 in range(0, N_B, heads_per_group):
        heads = list(range(g0, g0 + heads_per_group))
        qs = [q_ref[:, h * HEAD_DIM:(h + 1) * HEAD_DIM].astype(BF) for h in heads]

        def step(g, carry, masked):
            mask = (g * kw + k_off < q_pos) if masked else None
            st = [dict(c=c, acc=acc) for c, acc in carry]
            for i, s in _issue_order(len(heads), 3, SB_SKEW):
                h, d = heads[i], st[i]
                if s == 0:
                    d["ls"], d["lhs"] = _sb_pre(_dot(qs[i], _ktile(kT_ref, h, g, kw).astype(BF)), mask)
                elif s == 1:
                    d["a"], d["c"] = _sb_post(d["ls"], _dot(d["lhs"], rhs), d["c"], mask)
                else:
                    d["acc"] = d["acc"] + _dot_nt(d["a"].astype(BF), _ktile(vT_ref, h, g, kw).astype(BF))
            return tuple((d["c"], d["acc"]) for d in st)

        init = tuple((jnp.zeros((BLK, BLK), F32), jnp.zeros((BLK, HEAD_DIM), F32)) for _ in heads)
        carry = step(gd, init, True)
        carry = lax.fori_loop(0, gd, lambda i, cr: step(gd - 1 - i, cr, False), carry)
        for _, acc in carry:
            outs.append(acc)
    o_ref[...] = jnp.concatenate(outs, axis=1)


def _sb_prompt(q, kT, vT, *, heads_per_group=N_B):
    B, S, W = q.shape
    kw = min(K_TILE_BLOCKS * BLK, S)
    return pl.pallas_call(
        functools.partial(_sb_prompt_kernel, heads_per_group=heads_per_group, kw=kw),
        grid=(B, S // BLK),
        in_specs=[pl.BlockSpec((None, BLK, W), lambda b, i: (b, i, 0)),
                  pl.BlockSpec((None, W, S), lambda b, i: (b, 0, 0)),
                  pl.BlockSpec((None, W, S), lambda b, i: (b, 0, 0))],
        out_specs=pl.BlockSpec((None, BLK, W), lambda b, i: (b, i, 0)),
        out_shape=jax.ShapeDtypeStruct((B, S, W), F32),
        compiler_params=_cparams("parallel", "arbitrary"),
        name="sb_prompt",
    )(q, kT, vT)


def _stack_heads(parts):
    return jnp.concatenate(parts, axis=0)


def _decode_kernel(pt_ref, qb_ref, qc_ref, kb_ref, vb_ref, kc_ref, vc_ref, lf_ref, *refs, n_pages, t_new):
    del pt_ref
    n = n_pages
    sbk, sbv, fk, fv, flf = (refs[i * n:(i + 1) * n] for i in range(5))
    ob_ref, oc_ref = refs[5 * n], refs[5 * n + 1]
    T = t_new
    rows = N_B * T
    hs = lambda x, h: x[:, h * HEAD_DIM:(h + 1) * HEAD_DIM]

    sb_rhs = _suffix_sum_rhs()
    kk = lax.broadcasted_iota(jnp.int32, (3 * PAGE, 2 * PAGE), 0) % PAGE
    nn = lax.broadcasted_iota(jnp.int32, (3 * PAGE, 2 * PAGE), 1)
    lf_rhs = jnp.where((nn >= PAGE) | (kk > nn), 1.0, 0.0).astype(BF)
    t_idx = lax.broadcasted_iota(jnp.int32, (rows, PAGE), 0) % T
    s_idx = lax.broadcasted_iota(jnp.int32, (rows, PAGE), 1)
    zpad = jnp.zeros((PAGE - T, HEAD_DIM), BF)
    ones_rows = jnp.ones((SUBLANE, PAGE), BF)

    page = lambda x, p: x[:, p * PAGE:(p + 1) * PAGE]

    qb = qb_ref[...].astype(BF)
    kb_new = kb_ref[...].astype(BF)
    vb_new = vb_ref[...].astype(BF)

    lf_new = lf_ref[...]
    g_rows = [lf_new[0:1]]
    for t in range(1, T):
        g_rows.append(g_rows[-1] + lf_new[t:t + 1])
    g_col = jnp.concatenate(g_rows, axis=0)
    eye = lax.broadcasted_iota(jnp.int32, (T, PAGE), 0) == lax.broadcasted_iota(jnp.int32, (T, PAGE), 1)
    g_q = _stack_heads([jnp.broadcast_to(g_col[:, h:h + 1], (T, PAGE)) for h in range(N_C)])
    g_k = _stack_heads([jnp.broadcast_to(
        jnp.sum(jnp.where(eye, jnp.broadcast_to(g_col[:, h:h + 1], (T, PAGE)), 0.0), axis=0, keepdims=True),
        (T, PAGE)) for h in range(N_C)])
    qc = qc_ref[...].astype(BF)
    kc_new = kc_ref[...].astype(BF)
    vc_new = vc_ref[...].astype(BF)

    zb_new = _stack_heads([_dot_nt(hs(qb, h), jnp.concatenate([hs(kb_new, h), zpad], axis=0)) for h in range(N_B)])
    zb = jnp.concatenate([_stack_heads([_dot(hs(qb, h), sbk[p][h].astype(BF)) for h in range(N_B)])
                          for p in range(n)], axis=1)

    lf_pad = jnp.zeros((SUBLANE - N_C, PAGE), F32)
    lf_all = jnp.concatenate([jnp.concatenate([flf[p][...], lf_pad], axis=0) for p in range(n)], axis=0)
    cs = _dot(jnp.concatenate(_split3(lf_all), axis=1), lf_rhs)
    zc_new = _stack_heads([_dot_nt(hs(qc, h), jnp.concatenate([hs(kc_new, h), zpad], axis=0)) for h in range(N_C)])
    zc_pages = [_stack_heads([_dot(hs(qc, h), fk[p][h].astype(BF)) for h in range(N_C)]) for p in range(n)]

    mask_new = s_idx < t_idx
    ls_new, lhs_new = _sb_pre(zb_new, mask_new)
    ls_all, lhs_all = _sb_pre(zb, None)
    cs_new = _dot(lhs_new, sb_rhs)
    cs_all = _dot(lhs_all, sb_rhs)

    zc_new = jnp.where(s_idx <= t_idx, zc_new + g_q - g_k, NEG)
    r_carry = jnp.zeros((SUBLANE, PAGE), F32)
    for p in reversed(range(n)):
        csp = cs[p * SUBLANE:(p + 1) * SUBLANE]
        r_page = csp[:, :PAGE] + r_carry
        r_carry = r_carry + csp[:, PAGE:]
        bias = _stack_heads([jnp.broadcast_to(r_page[h:h + 1], (T, PAGE)) for h in range(N_C)]) + g_q
        zc_pages[p] = zc_pages[p] + bias
    zc = jnp.concatenate(zc_pages, axis=1)
    m = jnp.maximum(jnp.max(zc_new, axis=1, keepdims=True), jnp.max(zc, axis=1, keepdims=True))
    p_new = jnp.exp(zc_new - m)
    p_all = jnp.exp(zc - m)
    den = jnp.sum(p_new, axis=1, keepdims=True) + jnp.sum(p_all, axis=1, keepdims=True)
    p_new = p_new.astype(BF)
    p_all = p_all.astype(BF)
    accs = [_dot(p_new[h * T:(h + 1) * T], jnp.concatenate([hs(vc_new, h), zpad], axis=0)) for h in range(N_C)]
    for p in range(n):
        accs = [accs[h] + _dot_nt(page(p_all, p)[h * T:(h + 1) * T], fv[p][h].astype(BF)) for h in range(N_C)]

    a_new, c = _sb_post(ls_new, cs_new, jnp.zeros((rows, PAGE), F32), mask_new)
    a_all, _ = _sb_post(ls_all, cs_all, c, None)
    a_new = a_new.astype(BF)
    a_all = a_all.astype(BF)
    acc = [_dot(a_new[h * T:(h + 1) * T], jnp.concatenate([hs(vb_new, h), zpad], axis=0)) for h in range(N_B)]
    for p in range(n):
        acc = [acc[h] + _dot_nt(page(a_all, p)[h * T:(h + 1) * T], sbv[p][h].astype(BF)) for h in range(N_B)]

    out = _stack_heads(accs) / den
    oc_ref[...] = jnp.concatenate([out[h * T:(h + 1) * T] for h in range(N_C)], axis=1)
    ob_ref[...] = jnp.concatenate(acc, axis=1)


def _decode(page_table, qb, qc, kb, vb, kc, vc, lf, sbkT, sbvT, fkT, fvT, flfT, *, layer, t_new):
    nb, n_pages = page_table.shape
    N = qb.shape[0]
    tok = lambda w: pl.BlockSpec((t_new, w), lambda b, pt: (b, 0))
    kv_specs = [pl.BlockSpec((None, None, N_B, HEAD_DIM, PAGE), functools.partial(
        lambda b, pt, p: (layer, pt[b, p], 0, 0, 0), p=p)) for p in range(n_pages)]
    lf_specs = [pl.BlockSpec((None, None, N_C, PAGE), functools.partial(
        lambda b, pt, p: (layer, pt[b, p], 0, 0), p=p)) for p in range(n_pages)]
    return pl.pallas_call(
        functools.partial(_decode_kernel, n_pages=n_pages, t_new=t_new),
        grid_spec=pltpu.PrefetchScalarGridSpec(
            num_scalar_prefetch=1,
            grid=(nb,),
            in_specs=[tok(W_B), tok(W_C), tok(W_B), tok(W_B), tok(W_C), tok(W_C), tok(LANE)]
                     + kv_specs * 4 + lf_specs,
            out_specs=[tok(W_B), tok(W_C)],
        ),
        out_shape=[jax.ShapeDtypeStruct((N, W_B), F32), jax.ShapeDtypeStruct((N, W_C), F32)],
        compiler_params=_cparams("parallel"),
        name="decode_attention",
    )(page_table, qb, qc, kb, vb, kc, vc, lf,
      *([sbkT] * n_pages), *([sbvT] * n_pages), *([fkT] * n_pages), *([fvT] * n_pages), *([flfT] * n_pages))


TM_PROMPT = 512
TM_SAMPLE = 256


def kernel(x_prompt, x_sample, cache_sb_k, cache_sb_v, cache_fox_k, cache_fox_v, cache_fox_logf, page_table, g_attn, w_in, b_f, g_v, w_s, b_s, g_mix, w_o, g_ffn, w_ffn_in, w_ffn_out, g_final):
    B, S, D = x_prompt.shape
    DB, T, _ = x_sample.shape
    depth = g_attn.shape[0]
    assert D == D_MODEL and S % TM_PROMPT == 0 and (DB * T) % TM_SAMPLE == 0
    assert CHUNK % T == 0 and TM_SAMPLE % CHUNK == 0 and cache_sb_k.shape[2] == PAGE

    kv_t = lambda c: jnp.transpose(c, (0, 1, 3, 4, 2))
    sbkT, sbvT, fkT, fvT = kv_t(cache_sb_k), kv_t(cache_sb_v), kv_t(cache_fox_k), kv_t(cache_fox_v)
    flfT = jnp.transpose(cache_fox_logf, (0, 1, 3, 2))

    yp = x_prompt
    ys = x_sample.reshape(DB * T, D)
    outs_p = [[] for _ in range(5)]
    outs_s = [[] for _ in range(6)]
    for l in range(depth):
        last = l == depth - 1
        w_tok, w_kv, w_kvT, bf_col, bf_row = _split_w_in(w_in[l], b_f[l])
        wo, wfi, wfo = w_o[l].astype(BF), w_ffn_in[l].astype(BF), w_ffn_out[l].astype(BF)

        u, va, qb, qc, lfc, kbT, vbT, kcT, vcT, lfr = _inproj_prompt(
            yp, g_attn[l], w_tok, w_kvT, bf_col, bf_row, g_v[l], tm=TM_PROMPT)
        f_row, f_col = _fcum(lfr, lfc)
        c_out = _fox_prompt(qc, kcT, vcT, f_row, f_col)
        b_out = _sb_prompt(qb, kbT, vbT)
        wmix, bmix = _mix_operands_prompt(w_s[l], b_s[l])
        flat = lambda a: a.reshape(B * S, a.shape[-1])
        y1 = _merge(flat(yp), flat(u), flat(va), flat(b_out), flat(c_out), wmix, bmix, g_mix[l], wo, tm=TM_PROMPT)
        yp = _ffn(y1, g_ffn[l], wfi, wfo, g_final, tm=TM_PROMPT, final_norm=last).reshape(B, S, D)
        for dst, a in zip(outs_p, (kbT, vbT, kcT, vcT)):
            dst.append(a.reshape(B, N_B, HEAD_DIM, S))
        outs_p[4].append(lfr[:, :N_C, :])

        u, va, qb, qc, lfc, kb, vb, kc, vc = _inproj_sample(
            ys, g_attn[l], w_tok, w_kv, bf_col, g_v[l], tm=TM_SAMPLE)
        b_out, c_out = _decode(page_table, qb, qc, kb, vb, kc, vc, lfc, sbkT, sbvT, fkT, fvT, flfT,
                               layer=l, t_new=T)
        wmix, bmix = _mix_operands_sample(w_s[l], b_s[l], T)
        y1 = _merge(ys, u, va, b_out, c_out, wmix, bmix, g_mix[l], wo, tm=TM_SAMPLE)
        ys = _ffn(y1, g_ffn[l], wfi, wfo, g_final, tm=TM_SAMPLE, final_norm=last)
        for dst, a in zip(outs_s, (kb, vb, kc, vc)):
            dst.append(a.reshape(DB, T, N_B, HEAD_DIM))
        outs_s[4].append(lfc[:, :N_C].reshape(DB, T, N_C))
        outs_s[5].append(va.reshape(DB, T, W_A))

    to_seq_major = lambda xs: jnp.transpose(jnp.stack(xs), (0, 1, 4, 2, 3))
    p_lf = jnp.transpose(jnp.stack(outs_p[4]), (0, 1, 3, 2))
    return (yp, ys.reshape(DB, T, D),
            to_seq_major(outs_p[0]), to_seq_major(outs_p[1]), to_seq_major(outs_p[2]), to_seq_major(outs_p[3]), p_lf,
            jnp.stack(outs_s[0]), jnp.stack(outs_s[1]), jnp.stack(outs_s[2]), jnp.stack(outs_s[3]),
            jnp.stack(outs_s[4]), jnp.stack(outs_s[5]))
```

```python
import functools
import math

import jax
import jax.numpy as jnp
import numpy as np
from jax import lax
from jax.experimental import pallas as pl
from jax.experimental.pallas import tpu as pltpu

D_MODEL = 1024
HEAD_DIM = 64
N_A, N_B, N_C = 4, 6, 6
W_A, W_B, W_C = N_A * HEAD_DIM, N_B * HEAD_DIM, N_C * HEAD_DIM
CHUNK = 128
PAGE = 128
D_FF = 2816
EPS = 1e-6
NEG = -1e30
LANE = 128
SUBLANE = 8
BLK = 128
VMEM_LIMIT = 48 * 1024 * 1024
BF = jnp.bfloat16
F32 = jnp.float32

_O_U, _O_VA, _O_QB, _O_KB, _O_VB = 0, W_A, 2 * W_A, 2 * W_A + W_B, 2 * W_A + 2 * W_B
_O_QC = 2 * W_A + 3 * W_B
_O_KC, _O_VC, _O_F = _O_QC + W_C, _O_QC + 2 * W_C, _O_QC + 3 * W_C


def _cparams(*sem):
    return pltpu.CompilerParams(dimension_semantics=sem, vmem_limit_bytes=VMEM_LIMIT)


def _gelu(x):
    return 0.5 * x * (1.0 + jnp.tanh(math.sqrt(2.0 / math.pi) * (x + 0.044715 * (x * x * x))))


def _log_sigmoid(x):
    return jnp.minimum(x, 0.0) - jnp.log(1.0 + jnp.exp(-jnp.abs(x)))


def _rms(x, g):
    return x * lax.rsqrt(jnp.mean(x * x, axis=-1, keepdims=True) + EPS) * g


def _dot(a, b):
    return jnp.dot(a, b, preferred_element_type=F32)


def _dot_nt(a, b):
    return lax.dot_general(a, b, (((1,), (1,)), ((), ())), preferred_element_type=F32)


def _inproj_tok_epilogue(zt, bf_col, gv, u_ref, va_ref, qb_ref, qc_ref, lfc_ref):
    u_ref[...] = _gelu(zt[:, 0:W_A])
    va_ref[...] = _rms(_gelu(zt[:, W_A:2 * W_A]), gv)
    qb_ref[...] = zt[:, 2 * W_A:2 * W_A + W_B] * (HEAD_DIM ** -0.5)
    qc_ref[...] = zt[:, 2 * W_A + W_B:2 * W_A + W_B + W_C] * (HEAD_DIM ** -0.5)
    o = 2 * W_A + W_B + W_C
    lfc_ref[...] = _log_sigmoid(zt[:, o:o + LANE] + bf_col)


def _inproj_prompt_kernel(x_ref, g_ref, wt_ref, wT_ref, bfc_ref, bfr_ref, gv_ref,
                          u_ref, va_ref, qb_ref, qc_ref, lfc_ref,
                          kbT_ref, vbT_ref, kcT_ref, vcT_ref, lfr_ref):
    xn = _rms(x_ref[...], g_ref[...]).astype(BF)
    zt = _dot_nt(xn, wt_ref[...])
    _inproj_tok_epilogue(zt, bfc_ref[...], gv_ref[...], u_ref, va_ref, qb_ref, qc_ref, lfc_ref)
    zT = _dot_nt(wT_ref[...], xn)
    kbT_ref[...] = zT[0:W_B]
    vbT_ref[...] = zT[W_B:2 * W_B]
    kcT_ref[...] = zT[2 * W_B:2 * W_B + W_C]
    vcT_ref[...] = zT[2 * W_B + W_C:2 * W_B + 2 * W_C]
    o = 2 * W_B + 2 * W_C
    lfr_ref[...] = _log_sigmoid(zT[o:o + SUBLANE] + bfr_ref[...])


def _inproj_sample_kernel(x_ref, g_ref, wt_ref, wkv_ref, bfc_ref, gv_ref,
                          u_ref, va_ref, qb_ref, qc_ref, lfc_ref,
                          kb_ref, vb_ref, kc_ref, vc_ref):
    xn = _rms(x_ref[...], g_ref[...]).astype(BF)
    zt = _dot_nt(xn, wt_ref[...])
    _inproj_tok_epilogue(zt, bfc_ref[...], gv_ref[...], u_ref, va_ref, qb_ref, qc_ref, lfc_ref)
    zk = _dot_nt(xn, wkv_ref[0:2 * W_B + 2 * W_C, :])
    kb_ref[...] = zk[:, 0:W_B]
    vb_ref[...] = zk[:, W_B:2 * W_B]
    kc_ref[...] = zk[:, 2 * W_B:2 * W_B + W_C]
    vc_ref[...] = zk[:, 2 * W_B + W_C:2 * W_B + 2 * W_C]


def _split_w_in(w_in_l, b_f_l):
    wT = w_in_l.T
    wfT = wT[_O_F:]
    w_tokT = jnp.concatenate([wT[_O_U:_O_QB], wT[_O_QB:_O_KB], wT[_O_QC:_O_KC],
                              jnp.pad(wfT, ((0, LANE - N_C), (0, 0)))], axis=0).astype(BF)
    w_kvT = jnp.concatenate([wT[_O_KB:_O_QC], wT[_O_KC:_O_F],
                             jnp.pad(wfT, ((0, SUBLANE - N_C), (0, 0)))], axis=0).astype(BF)
    bf_col = jnp.pad(b_f_l, (0, LANE - N_C)).reshape(1, LANE)
    bf_row = jnp.pad(b_f_l, (0, SUBLANE - N_C)).reshape(SUBLANE, 1)
    return w_tokT, w_kvT, bf_col, bf_row


def _const_spec(shape):
    nd = len(shape)
    return pl.BlockSpec(shape, lambda *_: (0,) * nd, pipeline_mode=pl.Buffered(1))


def _inproj_prompt(x, g, w_tokT, w_kvT, bf_col, bf_row, gv, *, tm):
    B, S, D = x.shape
    row = lambda w: pl.BlockSpec((None, tm, w), lambda b, s: (b, s, 0))
    colT = lambda h: pl.BlockSpec((None, h, tm), lambda b, s: (b, 0, s))
    f = lambda *sh: jax.ShapeDtypeStruct(sh, F32)
    return pl.pallas_call(
        _inproj_prompt_kernel,
        grid=(B, S // tm),
        in_specs=[row(D), _const_spec((1, D)), _const_spec(w_tokT.shape), _const_spec(w_kvT.shape),
                  _const_spec((1, LANE)), _const_spec((SUBLANE, 1)), _const_spec((1, W_A))],
        out_specs=[row(W_A), row(W_A), row(W_B), row(W_C), row(LANE),
                   colT(W_B), colT(W_B), colT(W_C), colT(W_C), colT(SUBLANE)],
        out_shape=[f(B, S, W_A), f(B, S, W_A), f(B, S, W_B), f(B, S, W_C), f(B, S, LANE),
                   f(B, W_B, S), f(B, W_B, S), f(B, W_C, S), f(B, W_C, S), f(B, SUBLANE, S)],
        compiler_params=_cparams("parallel", "parallel"),
        name="inproj_prompt",
    )(x, g.reshape(1, D), w_tokT, w_kvT, bf_col, bf_row, gv.reshape(1, W_A))


def _inproj_sample(x, g, w_tokT, w_kvT, bf_col, gv, *, tm):
    N, D = x.shape
    row = lambda w: pl.BlockSpec((tm, w), lambda i: (i, 0))
    f = lambda *sh: jax.ShapeDtypeStruct(sh, F32)
    return pl.pallas_call(
        _inproj_sample_kernel,
        grid=(N // tm,),
        in_specs=[row(D), _const_spec((1, D)), _const_spec(w_tokT.shape), _const_spec(w_kvT.shape),
                  _const_spec((1, LANE)), _const_spec((1, W_A))],
        out_specs=[row(W_A), row(W_A), row(W_B), row(W_C), row(LANE),
                   row(W_B), row(W_B), row(W_C), row(W_C)],
        out_shape=[f(N, W_A), f(N, W_A), f(N, W_B), f(N, W_C), f(N, LANE),
                   f(N, W_B), f(N, W_B), f(N, W_C), f(N, W_C)],
        compiler_params=_cparams("parallel"),
        name="inproj_sample",
    )(x, g.reshape(1, D), w_tokT, w_kvT, bf_col, gv.reshape(1, W_A))


def _merge_kernel(x_ref, u_ref, va_ref, b_ref, c_ref, wmix_ref, bmix_ref, gmix_ref, wo_ref, y_ref, *, tm):
    lane_grp = lax.broadcasted_iota(jnp.int32, (CHUNK, W_A), 1) // HEAD_DIM
    a_rows = []
    for r in range(tm // CHUNK):
        va = va_ref[r * CHUNK:(r + 1) * CHUNK, :]
        mixed = bmix_ref[...]
        for g in range(N_A):
            vg = jnp.where(lane_grp == g, va, 0.0).astype(BF)
            mixed = mixed + _dot(wmix_ref[g], vg)
        a_rows.append(u_ref[r * CHUNK:(r + 1) * CHUNK, :] * mixed)
    a_out = jnp.concatenate(a_rows, axis=0) if len(a_rows) > 1 else a_rows[0]
    gm = gmix_ref[...]
    cat = jnp.concatenate([_rms(a_out, gm[:, 0:W_A]).astype(BF),
                           _rms(b_ref[...], gm[:, W_A:W_A + W_B]).astype(BF),
                           _rms(c_ref[...], gm[:, W_A + W_B:]).astype(BF)], axis=1)
    y_ref[...] = x_ref[...] + _dot(cat, wo_ref[...])


def _mix_operands_prompt(w_s_l, b_s_l):
    wmix = jnp.tril(w_s_l).astype(BF)
    bmix = jnp.repeat(b_s_l.T, HEAD_DIM, axis=1)
    return wmix, bmix


def _mix_operands_sample(w_s_l, b_s_l, t):
    reps = CHUNK // t
    small = jnp.tril(w_s_l[:, :t, :t])
    wmix = jax.vmap(lambda m: jnp.kron(jnp.eye(reps, dtype=m.dtype), m))(small).astype(BF)
    bmix = jnp.tile(jnp.repeat(b_s_l[:, :t].T, HEAD_DIM, axis=1), (reps, 1))
    return wmix, bmix


def _merge(x, u, va, b_out, c_out, wmix, bmix, gmix, wo, *, tm):
    N, D = x.shape
    row = lambda w: pl.BlockSpec((tm, w), lambda i: (i, 0))
    return pl.pallas_call(
        functools.partial(_merge_kernel, tm=tm),
        grid=(N // tm,),
        in_specs=[row(D), row(W_A), row(W_A), row(W_B), row(W_C),
                  _const_spec((N_A, CHUNK, CHUNK)), _const_spec((CHUNK, W_A)),
                  _const_spec((1, D)), _const_spec((D, D))],
        out_specs=row(D),
        out_shape=jax.ShapeDtypeStruct((N, D), F32),
        compiler_params=_cparams("parallel"),
        name="merge",
    )(x, u, va, b_out, c_out, wmix, bmix, gmix.reshape(1, D), wo)


FF_CHUNK = 256


def _ffn_kernel(x_ref, g_ref, win_ref, wout_ref, gfin_ref, y_ref, *, final_norm):
    x = x_ref[...]
    xn = _rms(x, g_ref[...]).astype(BF)
    acc = x
    for c in range(D_FF // FF_CHUNK):
        h = _dot(xn, win_ref[:, c * FF_CHUNK:(c + 1) * FF_CHUNK])
        gate = _dot(xn, win_ref[:, D_FF + c * FF_CHUNK:D_FF + (c + 1) * FF_CHUNK])
        act = (h * jax.nn.sigmoid(h) * gate).astype(BF)
        acc = acc + _dot(act, wout_ref[c * FF_CHUNK:(c + 1) * FF_CHUNK, :])
    if final_norm:
        acc = _rms(acc, gfin_ref[...])
    y_ref[...] = acc


def _ffn(x, g, win, wout, gfin, *, tm, final_norm):
    N, D = x.shape
    row = pl.BlockSpec((tm, D), lambda i: (i, 0))
    return pl.pallas_call(
        functools.partial(_ffn_kernel, final_norm=final_norm),
        grid=(N // tm,),
        in_specs=[row, _const_spec((1, D)), _const_spec((D, 2 * D_FF)), _const_spec((D_FF, D)),
                  _const_spec((1, D))],
        out_specs=row,
        out_shape=jax.ShapeDtypeStruct((N, D), F32),
        compiler_params=_cparams("parallel"),
        name="ffn",
    )(x, g.reshape(1, D), win, wout, gfin.reshape(1, D))


def _split3(x):
    hi = x.astype(BF)
    r = x - hi.astype(F32)
    mid = r.astype(BF)
    lo = (r - mid.astype(F32)).astype(BF)
    return hi, mid, lo


def _fcum_kernel(lfr_ref, lfc_ref, fr_ref, fc_ref, *, S):
    ii = lax.broadcasted_iota(jnp.int32, (BLK, BLK), 0)
    jj = lax.broadcasted_iota(jnp.int32, (BLK, BLK), 1)
    t_incl = jnp.where(ii <= jj, 1.0, 0.0).astype(BF)
    l_incl = jnp.where(jj <= ii, 1.0, 0.0).astype(BF)
    rhs1 = jnp.concatenate([t_incl, jnp.ones((BLK, BLK), BF)], axis=1)
    rhs = jnp.concatenate([rhs1, rhs1, rhs1], axis=0)
    carry_r = jnp.zeros((SUBLANE, BLK), F32)
    carry_c = jnp.zeros((1, LANE), F32)
    for blk in range(S // BLK):
        sl = slice(blk * BLK, (blk + 1) * BLK)
        cs = _dot(jnp.concatenate(_split3(lfr_ref[:, sl]), axis=1), rhs)
        fr_ref[:, sl] = cs[:, :BLK] + carry_r
        carry_r = carry_r + cs[:, BLK:]
        hi, mid, lo = _split3(lfc_ref[sl, :])
        fc = _dot(l_incl, hi) + _dot(l_incl, mid) + _dot(l_incl, lo) + carry_c
        fc_ref[sl, :] = fc
        carry_c = fc[BLK - 1:BLK, :]


def _fcum(lf_row, lf_col):
    B, _, S = lf_row.shape
    return pl.pallas_call(
        functools.partial(_fcum_kernel, S=S),
        grid=(B,),
        in_specs=[pl.BlockSpec((None, SUBLANE, S), lambda b: (b, 0, 0)),
                  pl.BlockSpec((None, S, LANE), lambda b: (b, 0, 0))],
        out_specs=[pl.BlockSpec((None, SUBLANE, S), lambda b: (b, 0, 0)),
                   pl.BlockSpec((None, S, LANE), lambda b: (b, 0, 0))],
        out_shape=[jax.ShapeDtypeStruct((B, SUBLANE, S), F32), jax.ShapeDtypeStruct((B, S, LANE), F32)],
        compiler_params=_cparams("parallel"),
        name="fox_cumsum",
    )(lf_row, lf_col)


K_TILE_BLOCKS = 4


def _ktile(ref, h, start, width):
    return ref[h * HEAD_DIM:(h + 1) * HEAD_DIM, pl.ds(pl.multiple_of(start, BLK), width)]


def _fox_prompt_kernel(q_ref, kT_ref, vT_ref, fr_ref, fc_ref, o_ref, *, heads_per_group, kw):
    qi = pl.program_id(1)
    nblk = kw // BLK
    gd = qi // nblk
    causal = lax.broadcasted_iota(jnp.int32, (BLK, BLK), 1) <= lax.broadcasted_iota(jnp.int32, (BLK, BLK), 0)
    outs = []
    for g0 in range(0, N_C, heads_per_group):
        heads = list(range(g0, g0 + heads_per_group))
        qs = [q_ref[:, h * HEAD_DIM:(h + 1) * HEAD_DIM].astype(BF) for h in heads]
        fcols = [fc_ref[:, h:h + 1] for h in heads]

        def step(g, carry, width, mask_last):
            ones_rows = jnp.ones((SUBLANE, width), BF)
            st = [dict(m=m, acc=acc) for m, acc in carry]
            for i, s in _issue_order(len(heads), 2, FOX_SKEW):
                h, d = heads[i], st[i]
                if s == 0:
                    frow = fr_ref[h:h + 1, pl.ds(pl.multiple_of(g * kw, BLK), width)]
                    z = _dot(qs[i], _ktile(kT_ref, h, g * kw, width).astype(BF)) + fcols[i] - frow
                    if mask_last is not None:
                        z_last = jnp.where(mask_last, z[:, width - BLK:], NEG)
                        z = jnp.concatenate([z[:, :width - BLK], z_last], axis=1) if width > BLK else z_last
                    m_new = jnp.maximum(d["m"], jnp.max(z, axis=1, keepdims=True))
                    d["p"] = jnp.exp(z - m_new).astype(BF)
                    d["alpha"] = jnp.exp(d["m"] - m_new)
                    d["m"] = m_new
                else:
                    v_ext = jnp.concatenate([_ktile(vT_ref, h, g * kw, width).astype(BF), ones_rows], axis=0)
                    d["acc"] = d["acc"] * d["alpha"] + _dot_nt(d["p"], v_ext)
            return tuple((d["m"], d["acc"]) for d in st)

        init = tuple((jnp.full((BLK, 1), NEG, F32), jnp.zeros((BLK, HEAD_DIM + SUBLANE), F32)) for _ in heads)
        carry = lax.fori_loop(0, gd, lambda g, c: step(g, c, kw, None), init)
        diag = [functools.partial(step, gd, width=(w + 1) * BLK, mask_last=causal) for w in range(nblk)]
        carry = lax.switch(qi % nblk, diag, carry) if nblk > 1 else diag[0](carry)
        for m, acc in carry:
            outs.append(acc[:, :HEAD_DIM] / acc[:, HEAD_DIM:HEAD_DIM + 1])
    o_ref[...] = jnp.concatenate(outs, axis=1)


def _fox_prompt(q, kT, vT, f_row, f_col, *, heads_per_group=N_C):
    B, S, W = q.shape
    kw = min(K_TILE_BLOCKS * BLK, S)
    return pl.pallas_call(
        functools.partial(_fox_prompt_kernel, heads_per_group=heads_per_group, kw=kw),
        grid=(B, S // BLK),
        in_specs=[pl.BlockSpec((None, BLK, W), lambda b, i: (b, i, 0)),
                  pl.BlockSpec((None, W, S), lambda b, i: (b, 0, 0)),
                  pl.BlockSpec((None, W, S), lambda b, i: (b, 0, 0)),
                  pl.BlockSpec((None, SUBLANE, S), lambda b, i: (b, 0, 0)),
                  pl.BlockSpec((None, BLK, LANE), lambda b, i: (b, i, 0))],
        out_specs=pl.BlockSpec((None, BLK, W), lambda b, i: (b, i, 0)),
        out_shape=jax.ShapeDtypeStruct((B, S, W), F32),
        compiler_params=_cparams("parallel", "arbitrary"),
        name="fox_prompt",
    )(q, kT, vT, f_row, f_col)


def _suffix_sum_rhs():
    k = lax.broadcasted_iota(jnp.int32, (2 * BLK, 2 * BLK), 0) % BLK
    n = lax.broadcasted_iota(jnp.int32, (2 * BLK, 2 * BLK), 1)
    return jnp.where((n >= BLK) | (k > n), 1.0, 0.0).astype(BF)


def _sb_pre(z, mask_last):
    nb = z.shape[1] // BLK
    sp = jnp.log(1.0 + jnp.exp(-jnp.abs(z)))
    ls = jnp.minimum(z, 0.0) - sp
    l1m = ls - z
    blk = lambda x, b: x[:, b * BLK:(b + 1) * BLK]
    lhs = []
    for b in range(nb):
        lb = blk(l1m, b)
        if mask_last is not None and b == nb - 1:
            lb = jnp.where(mask_last, lb, 0.0)
        hi = lb.astype(BF)
        lo = (lb - hi.astype(F32)).astype(BF)
        lhs.append(jnp.concatenate([hi, lo], axis=1))
    return ls, (jnp.concatenate(lhs, axis=0) if nb > 1 else lhs[0])


def _sb_post(ls, cs, c, mask_last):
    r, n = ls.shape
    nb = n // BLK
    a = [None] * nb
    for b in reversed(range(nb)):
        csb = cs[b * r:(b + 1) * r]
        ab = jnp.exp(ls[:, b * BLK:(b + 1) * BLK] + csb[:, :BLK] + c)
        if mask_last is not None and b == nb - 1:
            ab = jnp.where(mask_last, ab, 0.0)
        a[b] = ab.astype(BF)
        c = c + csb[:, BLK:]
    return (jnp.concatenate(a, axis=1) if nb > 1 else a[0]), c


def _issue_order(n_chains, n_stages, dist):
    items = [(h + s * dist, s, h) for h in range(n_chains) for s in range(n_stages)]
    return [(h, s) for _, s, h in sorted(items)]


SB_SKEW = 2
FOX_SKEW = 2


def _sb_prompt_kernel(q_ref, kT_ref, vT_ref, o_ref, *, heads_per_group, kw):
    qi = pl.program_id(1)
    nblk = kw // BLK
    gd = qi // nblk
    strict = lax.broadcasted_iota(jnp.int32, (BLK, BLK), 1) < lax.broadcasted_iota(jnp.int32, (BLK, BLK), 0)
    rhs = _suffix_sum_rhs()
    outs = []
    for g0 in range(0, N_B, heads_per_group):
        heads = list(range(g0, g0 + heads_per_group))
        qs = [q_ref[:, h * HEAD_DIM:(h + 1) * HEAD_DIM].astype(BF) for h in heads]

        def step(g, carry, width, mask_last):
            st = [dict(c=c, acc=acc) for c, acc in carry]
            for i, s in _issue_order(len(heads), 3, SB_SKEW):
                h, d = heads[i], st[i]
                if s == 0:
                    z = _dot(qs[i], _ktile(kT_ref, h, g * kw, width).astype(BF))
                    d["ls"], d["lhs"] = _sb_pre(z, mask_last)
                elif s == 1:
                    d["a"], d["c"] = _sb_post(d["ls"], _dot(d["lhs"], rhs), d["c"], mask_last)
                else:
                    d["acc"] = d["acc"] + _dot_nt(d["a"], _ktile(vT_ref, h, g * kw, width).astype(BF))
            return tuple((d["c"], d["acc"]) for d in st)

        init = tuple((jnp.zeros((BLK, BLK), F32), jnp.zeros((BLK, HEAD_DIM), F32)) for _ in heads)
        diag = [functools.partial(step, gd, init, (w + 1) * BLK, strict) for w in range(nblk)]
        carry = lax.switch(qi % nblk, diag) if nblk > 1 else diag[0]()
        carry = lax.fori_loop(0, gd, lambda i, cr: step(gd - 1 - i, cr, kw, None), carry)
        for _, acc in carry:
            outs.append(acc)
    o_ref[...] = jnp.concatenate(outs, axis=1)


def _sb_prompt(q, kT, vT, *, heads_per_group=N_B):
    B, S, W = q.shape
    kw = min(K_TILE_BLOCKS * BLK, S)
    return pl.pallas_call(
        functools.partial(_sb_prompt_kernel, heads_per_group=heads_per_group, kw=kw),
        grid=(B, S // BLK),
        in_specs=[pl.BlockSpec((None, BLK, W), lambda b, i: (b, i, 0)),
                  pl.BlockSpec((None, W, S), lambda b, i: (b, 0, 0)),
                  pl.BlockSpec((None, W, S), lambda b, i: (b, 0, 0))],
        out_specs=pl.BlockSpec((None, BLK, W), lambda b, i: (b, i, 0)),
        out_shape=jax.ShapeDtypeStruct((B, S, W), F32),
        compiler_params=_cparams("parallel", "arbitrary"),
        name="sb_prompt",
    )(q, kT, vT)


def _stack_heads(parts):
    return jnp.concatenate(parts, axis=0)


def _decode_kernel(pt_ref, qb_ref, qc_ref, kb_ref, vb_ref, kc_ref, vc_ref, lf_ref, *refs,
                   layer, n_pages, t_new, n_seq):
    n = n_pages
    (sbk_hbm, sbv_hbm, fk_hbm, fv_hbm, flf_hbm, ob_ref, oc_ref,
     sbk_buf, sbv_buf, fk_buf, fv_buf, flf_buf, sem) = refs
    b = pl.program_id(0)
    slot = lax.rem(b, 2)

    def page_copies(seq, dst_slot):
        cps = []
        for p in range(n):
            pg = pt_ref[seq, p]
            for src, dst in ((sbk_hbm, sbk_buf), (sbv_hbm, sbv_buf), (fk_hbm, fk_buf), (fv_hbm, fv_buf),
                             (flf_hbm, flf_buf)):
                cps.append(pltpu.make_async_copy(src.at[layer, pg], dst.at[dst_slot, p], sem.at[dst_slot]))
        return cps

    @pl.when(b == 0)
    def _():
        for cp in page_copies(0, 0):
            cp.start()

    nxt = jnp.minimum(b + 1, n_seq - 1)
    for cp in page_copies(nxt, 1 - slot):
        cp.start()
    for cp in page_copies(b, slot):
        cp.wait()

    sbk = [sbk_buf.at[slot, p] for p in range(n)]
    sbv = [sbv_buf.at[slot, p] for p in range(n)]
    fk = [fk_buf.at[slot, p] for p in range(n)]
    fv = [fv_buf.at[slot, p] for p in range(n)]
    flf = [flf_buf.at[slot, p] for p in range(n)]
    T = t_new
    rows = N_B * T
    hs = lambda x, h: x[:, h * HEAD_DIM:(h + 1) * HEAD_DIM]

    sb_rhs = _suffix_sum_rhs()
    kk = lax.broadcasted_iota(jnp.int32, (3 * PAGE, 2 * PAGE), 0) % PAGE
    nn = lax.broadcasted_iota(jnp.int32, (3 * PAGE, 2 * PAGE), 1)
    lf_rhs = jnp.where((nn >= PAGE) | (kk > nn), 1.0, 0.0).astype(BF)
    t_idx = lax.broadcasted_iota(jnp.int32, (rows, PAGE), 0) % T
    s_idx = lax.broadcasted_iota(jnp.int32, (rows, PAGE), 1)
    zpad = jnp.zeros((PAGE - T, HEAD_DIM), BF)
    ones_rows = jnp.ones((SUBLANE, PAGE), BF)

    page = lambda x, p: x[:, p * PAGE:(p + 1) * PAGE]

    qb = qb_ref[...].astype(BF)
    kb_new = kb_ref[...].astype(BF)
    vb_new = vb_ref[...].astype(BF)

    lf_new = lf_ref[...]
    g_rows = [lf_new[0:1]]
    for t in range(1, T):
        g_rows.append(g_rows[-1] + lf_new[t:t + 1])
    g_col = jnp.concatenate(g_rows, axis=0)
    eye = lax.broadcasted_iota(jnp.int32, (T, PAGE), 0) == lax.broadcasted_iota(jnp.int32, (T, PAGE), 1)
    g_q = _stack_heads([jnp.broadcast_to(g_col[:, h:h + 1], (T, PAGE)) for h in range(N_C)])
    g_k = _stack_heads([jnp.broadcast_to(
        jnp.sum(jnp.where(eye, jnp.broadcast_to(g_col[:, h:h + 1], (T, PAGE)), 0.0), axis=0, keepdims=True),
        (T, PAGE)) for h in range(N_C)])
    qc = qc_ref[...].astype(BF)
    kc_new = kc_ref[...].astype(BF)
    vc_new = vc_ref[...].astype(BF)

    zb_new = _stack_heads([_dot_nt(hs(qb, h), jnp.concatenate([hs(kb_new, h), zpad], axis=0)) for h in range(N_B)])
    zb = jnp.concatenate([_stack_heads([_dot(hs(qb, h), sbk[p][h].astype(BF)) for h in range(N_B)])
                          for p in range(n)], axis=1)

    lf_pad = jnp.zeros((SUBLANE - N_C, PAGE), F32)
    lf_all = jnp.concatenate([jnp.concatenate([flf[p][...], lf_pad], axis=0) for p in range(n)], axis=0)
    cs = _dot(jnp.concatenate(_split3(lf_all), axis=1), lf_rhs)
    zc_new = _stack_heads([_dot_nt(hs(qc, h), jnp.concatenate([hs(kc_new, h), zpad], axis=0)) for h in range(N_C)])
    zc_pages = [_stack_heads([_dot(hs(qc, h), fk[p][h].astype(BF)) for h in range(N_C)]) for p in range(n)]

    mask_new = s_idx < t_idx
    ls_new, lhs_new = _sb_pre(zb_new, mask_new)
    ls_all, lhs_all = _sb_pre(zb, None)
    cs_new = _dot(lhs_new, sb_rhs)
    cs_all = _dot(lhs_all, sb_rhs)

    zc_new = jnp.where(s_idx <= t_idx, zc_new + g_q - g_k, NEG)
    r_carry = jnp.zeros((SUBLANE, PAGE), F32)
    for p in reversed(range(n)):
        csp = cs[p * SUBLANE:(p + 1) * SUBLANE]
        r_page = csp[:, :PAGE] + r_carry
        r_carry = r_carry + csp[:, PAGE:]
        bias = _stack_heads([jnp.broadcast_to(r_page[h:h + 1], (T, PAGE)) for h in range(N_C)]) + g_q
        zc_pages[p] = zc_pages[p] + bias
    zc = jnp.concatenate(zc_pages, axis=1)
    m = jnp.maximum(jnp.max(zc_new, axis=1, keepdims=True), jnp.max(zc, axis=1, keepdims=True))
    p_new = jnp.exp(zc_new - m)
    p_all = jnp.exp(zc - m)
    den = jnp.sum(p_new, axis=1, keepdims=True) + jnp.sum(p_all, axis=1, keepdims=True)
    p_new = p_new.astype(BF)
    p_all = p_all.astype(BF)
    accs = [_dot(p_new[h * T:(h + 1) * T], jnp.concatenate([hs(vc_new, h), zpad], axis=0)) for h in range(N_C)]
    for p in range(n):
        accs = [accs[h] + _dot_nt(page(p_all, p)[h * T:(h + 1) * T], fv[p][h].astype(BF)) for h in range(N_C)]

    a_new, c = _sb_post(ls_new, cs_new, jnp.zeros((rows, PAGE), F32), mask_new)
    a_all, _ = _sb_post(ls_all, cs_all, c, None)
    a_new = a_new.astype(BF)
    a_all = a_all.astype(BF)
    acc = [_dot(a_new[h * T:(h + 1) * T], jnp.concatenate([hs(vb_new, h), zpad], axis=0)) for h in range(N_B)]
    for p in range(n):
        acc = [acc[h] + _dot_nt(page(a_all, p)[h * T:(h + 1) * T], sbv[p][h].astype(BF)) for h in range(N_B)]

    out = _stack_heads(accs) / den
    oc_ref[...] = jnp.concatenate([out[h * T:(h + 1) * T] for h in range(N_C)], axis=1)
    ob_ref[...] = jnp.concatenate(acc, axis=1)

    @pl.when(b == n_seq - 1)
    def _():
        for cp in page_copies(nxt, 1 - slot):
            cp.wait()


def _decode(page_table, qb, qc, kb, vb, kc, vc, lf, sbkT, sbvT, fkT, fvT, flfT, *, layer, t_new):
    nb, n_pages = page_table.shape
    N = qb.shape[0]
    tok = lambda w: pl.BlockSpec((t_new, w), lambda b, pt: (b, 0))
    hbm = pl.BlockSpec(memory_space=pl.ANY)
    kv_buf = pltpu.VMEM((2, n_pages, N_B, HEAD_DIM, PAGE), F32)
    return pl.pallas_call(
        functools.partial(_decode_kernel, layer=layer, n_pages=n_pages, t_new=t_new, n_seq=nb),
        grid_spec=pltpu.PrefetchScalarGridSpec(
            num_scalar_prefetch=1,
            grid=(nb,),
            in_specs=[tok(W_B), tok(W_C), tok(W_B), tok(W_B), tok(W_C), tok(W_C), tok(LANE)] + [hbm] * 5,
            out_specs=[tok(W_B), tok(W_C)],
            scratch_shapes=[kv_buf, kv_buf, kv_buf, kv_buf, pltpu.VMEM((2, n_pages, N_C, PAGE), F32),
                            pltpu.SemaphoreType.DMA((2,))],
        ),
        out_shape=[jax.ShapeDtypeStruct((N, W_B), F32), jax.ShapeDtypeStruct((N, W_C), F32)],
        compiler_params=_cparams("arbitrary"),
        name="decode_attention",
    )(page_table, qb, qc, kb, vb, kc, vc, lf, sbkT, sbvT, fkT, fvT, flfT)


TM_PROMPT = 512
TM_SAMPLE = 256


def kernel(x_prompt, x_sample, cache_sb_k, cache_sb_v, cache_fox_k, cache_fox_v, cache_fox_logf, page_table, g_attn, w_in, b_f, g_v, w_s, b_s, g_mix, w_o, g_ffn, w_ffn_in, w_ffn_out, g_final):
    B, S, D = x_prompt.shape
    DB, T, _ = x_sample.shape
    depth = g_attn.shape[0]
    assert D == D_MODEL and S % TM_PROMPT == 0 and (DB * T) % TM_SAMPLE == 0
    assert CHUNK % T == 0 and TM_SAMPLE % CHUNK == 0 and cache_sb_k.shape[2] == PAGE

    kv_t = lambda c: jnp.transpose(c, (0, 1, 3, 4, 2))
    sbkT, sbvT, fkT, fvT = kv_t(cache_sb_k), kv_t(cache_sb_v), kv_t(cache_fox_k), kv_t(cache_fox_v)
    flfT = jnp.transpose(cache_fox_logf, (0, 1, 3, 2))

    yp = x_prompt
    ys = x_sample.reshape(DB * T, D)
    outs_p = [[] for _ in range(5)]
    outs_s = [[] for _ in range(6)]
    for l in range(depth):
        last = l == depth - 1
        w_tokT, w_kvT, bf_col, bf_row = _split_w_in(w_in[l], b_f[l])
        wo, wfi, wfo = w_o[l].astype(BF), w_ffn_in[l].astype(BF), w_ffn_out[l].astype(BF)

        u, va, qb, qc, lfc, kbT, vbT, kcT, vcT, lfr = _inproj_prompt(
            yp, g_attn[l], w_tokT, w_kvT, bf_col, bf_row, g_v[l], tm=TM_PROMPT)
        f_row, f_col = _fcum(lfr, lfc)
        c_out = _fox_prompt(qc, kcT, vcT, f_row, f_col)
        b_out = _sb_prompt(qb, kbT, vbT)
        wmix, bmix = _mix_operands_prompt(w_s[l], b_s[l])
        flat = lambda a: a.reshape(B * S, a.shape[-1])
        y1 = _merge(flat(yp), flat(u), flat(va), flat(b_out), flat(c_out), wmix, bmix, g_mix[l], wo, tm=TM_PROMPT)
        yp = _ffn(y1, g_ffn[l], wfi, wfo, g_final, tm=TM_PROMPT, final_norm=last).reshape(B, S, D)
        for dst, a in zip(outs_p, (kbT, vbT, kcT, vcT)):
            dst.append(a.reshape(B, N_B, HEAD_DIM, S))
        outs_p[4].append(lfr[:, :N_C, :])

        u, va, qb, qc, lfc, kb, vb, kc, vc = _inproj_sample(
            ys, g_attn[l], w_tokT, w_kvT, bf_col, g_v[l], tm=TM_SAMPLE)
        b_out, c_out = _decode(page_table, qb, qc, kb, vb, kc, vc, lfc, sbkT, sbvT, fkT, fvT, flfT,
                               layer=l, t_new=T)
        wmix, bmix = _mix_operands_sample(w_s[l], b_s[l], T)
        y1 = _merge(ys, u, va, b_out, c_out, wmix, bmix, g_mix[l], wo, tm=TM_SAMPLE)
        ys = _ffn(y1, g_ffn[l], wfi, wfo, g_final, tm=TM_SAMPLE, final_norm=last)
        for dst, a in zip(outs_s, (kb, vb, kc, vc)):
            dst.append(a.reshape(DB, T, N_B, HEAD_DIM))
        outs_s[4].append(lfc[:, :N_C].reshape(DB, T, N_C))
        outs_s[5].append(va.reshape(DB, T, W_A))

    to_seq_major = lambda xs: jnp.transpose(jnp.stack(xs), (0, 1, 4, 2, 3))
    p_lf = jnp.transpose(jnp.stack(outs_p[4]), (0, 1, 3, 2))
    return (yp, ys.reshape(DB, T, D),
            to_seq_major(outs_p[0]), to_seq_major(outs_p[1]), to_seq_major(outs_p[2]), to_seq_major(outs_p[3]), p_lf,
            jnp.stack(outs_s[0]), jnp.stack(outs_s[1]), jnp.stack(outs_s[2]), jnp.stack(outs_s[3]),
            jnp.stack(outs_s[4]), jnp.stack(outs_s[5]))
```

```python
import functools
import math

import jax
import jax.numpy as jnp
import numpy as np
from jax import lax
from jax.experimental import pallas as pl
from jax.experimental.pallas import tpu as pltpu

D_MODEL = 1024
HEAD_DIM = 64
N_A, N_B, N_C = 4, 6, 6
W_A, W_B, W_C = N_A * HEAD_DIM, N_B * HEAD_DIM, N_C * HEAD_DIM
CHUNK = 128
PAGE = 128
D_FF = 2816
EPS = 1e-6
NEG = -1e30
LANE = 128
SUBLANE = 8
BLK = 128
VMEM_LIMIT = 48 * 1024 * 1024
BF = jnp.bfloat16
F32 = jnp.float32

_O_U, _O_VA, _O_QB, _O_KB, _O_VB = 0, W_A, 2 * W_A, 2 * W_A + W_B, 2 * W_A + 2 * W_B
_O_QC = 2 * W_A + 3 * W_B
_O_KC, _O_VC, _O_F = _O_QC + W_C, _O_QC + 2 * W_C, _O_QC + 3 * W_C


def _cparams(*sem):
    return pltpu.CompilerParams(dimension_semantics=sem, vmem_limit_bytes=VMEM_LIMIT)


def _gelu(x):
    return 0.5 * x * (1.0 + jnp.tanh(math.sqrt(2.0 / math.pi) * (x + 0.044715 * (x * x * x))))


def _log_sigmoid(x):
    return jnp.minimum(x, 0.0) - jnp.log(1.0 + jnp.exp(-jnp.abs(x)))


def _rms(x, g):
    return x * lax.rsqrt(jnp.mean(x * x, axis=-1, keepdims=True) + EPS) * g


def _dot(a, b):
    return jnp.dot(a, b, preferred_element_type=F32)


def _dot_nt(a, b):
    return lax.dot_general(a, b, (((1,), (1,)), ((), ())), preferred_element_type=F32)


def _inproj_tok_epilogue(zt, bf_col, gv, u_ref, va_ref, qb_ref, qc_ref, lfc_ref):
    u_ref[...] = _gelu(zt[:, 0:W_A])
    va_ref[...] = _rms(_gelu(zt[:, W_A:2 * W_A]), gv)
    qb_ref[...] = (zt[:, 2 * W_A:2 * W_A + W_B] * (HEAD_DIM ** -0.5)).astype(qb_ref.dtype)
    qc_ref[...] = (zt[:, 2 * W_A + W_B:2 * W_A + W_B + W_C] * (HEAD_DIM ** -0.5)).astype(qc_ref.dtype)
    o = 2 * W_A + W_B + W_C
    lfc_ref[...] = _log_sigmoid(zt[:, o:o + LANE] + bf_col)


def _inproj_prompt_kernel(x_ref, g_ref, wt_ref, wT_ref, bfc_ref, bfr_ref, gv_ref, *refs, layer, init_stack):
    if not init_stack:
        refs = refs[5:]
    u_ref, va_ref, qb_ref, qc_ref, lfc_ref, kbT_ref, vbT_ref, kcT_ref, vcT_ref, lfr_ref = refs

    def put(ref, val):
        if init_stack:
            for d in range(ref.shape[0]):
                ref[d] = val if d == layer else jnp.zeros_like(val)
        else:
            ref[...] = val

    xn = _rms(x_ref[...], g_ref[...]).astype(BF)
    zt = _dot_nt(xn, wt_ref[...])
    _inproj_tok_epilogue(zt, bfc_ref[...], gv_ref[...], u_ref, va_ref, qb_ref, qc_ref, lfc_ref)
    zT = _dot_nt(wT_ref[...], xn)
    put(kbT_ref, zT[0:W_B])
    put(vbT_ref, zT[W_B:2 * W_B])
    put(kcT_ref, zT[2 * W_B:2 * W_B + W_C])
    put(vcT_ref, zT[2 * W_B + W_C:2 * W_B + 2 * W_C])
    o = 2 * W_B + 2 * W_C
    put(lfr_ref, _log_sigmoid(zT[o:o + SUBLANE] + bfr_ref[...]))


def _inproj_sample_kernel(x_ref, g_ref, wt_ref, wkv_ref, bfc_ref, gv_ref,
                          u_ref, va_ref, qb_ref, qc_ref, lfc_ref,
                          kb_ref, vb_ref, kc_ref, vc_ref):
    xn = _rms(x_ref[...], g_ref[...]).astype(BF)
    zt = _dot_nt(xn, wt_ref[...])
    _inproj_tok_epilogue(zt, bfc_ref[...], gv_ref[...], u_ref, va_ref, qb_ref, qc_ref, lfc_ref)
    zk = _dot_nt(xn, wkv_ref[0:2 * W_B + 2 * W_C, :])
    kb_ref[...] = zk[:, 0:W_B]
    vb_ref[...] = zk[:, W_B:2 * W_B]
    kc_ref[...] = zk[:, 2 * W_B:2 * W_B + W_C]
    vc_ref[...] = zk[:, 2 * W_B + W_C:2 * W_B + 2 * W_C]


def _split_w_in(w_in_l, b_f_l):
    wT = w_in_l.T
    wfT = wT[_O_F:]
    w_tokT = jnp.concatenate([wT[_O_U:_O_QB], wT[_O_QB:_O_KB], wT[_O_QC:_O_KC],
                              jnp.pad(wfT, ((0, LANE - N_C), (0, 0)))], axis=0).astype(BF)
    w_kvT = jnp.concatenate([wT[_O_KB:_O_QC], wT[_O_KC:_O_F],
                             jnp.pad(wfT, ((0, SUBLANE - N_C), (0, 0)))], axis=0).astype(BF)
    bf_col = jnp.pad(b_f_l, (0, LANE - N_C)).reshape(1, LANE)
    bf_row = jnp.pad(b_f_l, (0, SUBLANE - N_C)).reshape(SUBLANE, 1)
    return w_tokT, w_kvT, bf_col, bf_row


def _const_spec(shape):
    nd = len(shape)
    return pl.BlockSpec(shape, lambda *_: (0,) * nd, pipeline_mode=pl.Buffered(1))


def _inproj_prompt(x, g, w_tokT, w_kvT, bf_col, bf_row, gv, stacked, *, layer, depth, tm):
    B, S, D = x.shape
    init_stack = stacked is None
    row = lambda w: pl.BlockSpec((None, tm, w), lambda b, s: (b, s, 0))
    if init_stack:
        colT = lambda h: pl.BlockSpec((depth, None, h, tm), lambda b, s: (0, b, 0, s))
    else:
        colT = lambda h: pl.BlockSpec((None, None, h, tm), lambda b, s: (layer, b, 0, s))
    f = lambda *sh: jax.ShapeDtypeStruct(sh, F32)
    n_in = 7
    return pl.pallas_call(
        functools.partial(_inproj_prompt_kernel, layer=layer, init_stack=init_stack),
        grid=(B, S // tm),
        in_specs=[row(D), _const_spec((1, D)), _const_spec(w_tokT.shape), _const_spec(w_kvT.shape),
                  _const_spec((1, LANE)), _const_spec((SUBLANE, 1)), _const_spec((1, W_A))]
                 + ([] if init_stack else [pl.BlockSpec(memory_space=pl.ANY)] * 5),
        out_specs=[row(W_A), row(W_A), row(W_B), row(W_C), row(LANE),
                   colT(W_B), colT(W_B), colT(W_C), colT(W_C), colT(SUBLANE)],
        out_shape=[f(B, S, W_A), f(B, S, W_A), jax.ShapeDtypeStruct((B, S, W_B), BF),
                   jax.ShapeDtypeStruct((B, S, W_C), BF), f(B, S, LANE),
                   f(depth, B, W_B, S), f(depth, B, W_B, S), f(depth, B, W_C, S), f(depth, B, W_C, S),
                   f(depth, B, SUBLANE, S)],
        input_output_aliases={} if init_stack else {n_in + j: 5 + j for j in range(5)},
        compiler_params=_cparams("parallel", "parallel"),
        name="inproj_prompt",
    )(x, g.reshape(1, D), w_tokT, w_kvT, bf_col, bf_row, gv.reshape(1, W_A), *([] if init_stack else stacked))


def _inproj_sample(x, g, w_tokT, w_kvT, bf_col, gv, *, tm):
    N, D = x.shape
    row = lambda w: pl.BlockSpec((tm, w), lambda i: (i, 0))
    f = lambda *sh: jax.ShapeDtypeStruct(sh, F32)
    return pl.pallas_call(
        _inproj_sample_kernel,
        grid=(N // tm,),
        in_specs=[row(D), _const_spec((1, D)), _const_spec(w_tokT.shape), _const_spec(w_kvT.shape),
                  _const_spec((1, LANE)), _const_spec((1, W_A))],
        out_specs=[row(W_A), row(W_A), row(W_B), row(W_C), row(LANE),
                   row(W_B), row(W_B), row(W_C), row(W_C)],
        out_shape=[f(N, W_A), f(N, W_A), f(N, W_B), f(N, W_C), f(N, LANE),
                   f(N, W_B), f(N, W_B), f(N, W_C), f(N, W_C)],
        compiler_params=_cparams("parallel"),
        name="inproj_sample",
    )(x, g.reshape(1, D), w_tokT, w_kvT, bf_col, gv.reshape(1, W_A))


FF_CHUNK = 256


def _merge_ffn_kernel(x_ref, u_ref, va_ref, b_ref, c_ref, wmix_ref, bmix_ref, gmix_ref, wo_ref,
                      gffn_ref, win_ref, wout_ref, gfin_ref, y_ref, *, tm, final_norm):
    lane_grp = lax.broadcasted_iota(jnp.int32, (CHUNK, W_A), 1) // HEAD_DIM
    a_rows = []
    for r in range(tm // CHUNK):
        va = va_ref[r * CHUNK:(r + 1) * CHUNK, :]
        mixed = bmix_ref[...]
        for g in range(N_A):
            vg = jnp.where(lane_grp == g, va, 0.0).astype(BF)
            mixed = mixed + _dot(wmix_ref[g], vg)
        a_rows.append(u_ref[r * CHUNK:(r + 1) * CHUNK, :] * mixed)
    a_out = jnp.concatenate(a_rows, axis=0) if len(a_rows) > 1 else a_rows[0]
    gm = gmix_ref[...]
    cat = jnp.concatenate([_rms(a_out, gm[:, 0:W_A]).astype(BF),
                           _rms(b_ref[...], gm[:, W_A:W_A + W_B]).astype(BF),
                           _rms(c_ref[...], gm[:, W_A + W_B:]).astype(BF)], axis=1)
    y = x_ref[...] + _dot(cat, wo_ref[...])
    yn = _rms(y, gffn_ref[...]).astype(BF)
    acc = y
    for c in range(D_FF // FF_CHUNK):
        h = _dot(yn, win_ref[:, c * FF_CHUNK:(c + 1) * FF_CHUNK])
        gate = _dot(yn, win_ref[:, D_FF + c * FF_CHUNK:D_FF + (c + 1) * FF_CHUNK])
        act = (h * jax.nn.sigmoid(h) * gate).astype(BF)
        acc = acc + _dot(act, wout_ref[c * FF_CHUNK:(c + 1) * FF_CHUNK, :])
    if final_norm:
        acc = _rms(acc, gfin_ref[...])
    y_ref[...] = acc


def _mix_operands_prompt(w_s_l, b_s_l):
    wmix = jnp.tril(w_s_l).astype(BF)
    bmix = jnp.repeat(b_s_l.T, HEAD_DIM, axis=1)
    return wmix, bmix


def _mix_operands_sample(w_s_l, b_s_l, t):
    reps = CHUNK // t
    small = jnp.tril(w_s_l[:, :t, :t])
    wmix = jax.vmap(lambda m: jnp.kron(jnp.eye(reps, dtype=m.dtype), m))(small).astype(BF)
    bmix = jnp.tile(jnp.repeat(b_s_l[:, :t].T, HEAD_DIM, axis=1), (reps, 1))
    return wmix, bmix


def _merge_ffn(x, u, va, b_out, c_out, wmix, bmix, gmix, wo, gffn, win, wout, gfin, *, tm, final_norm):
    N, D = x.shape
    row = lambda w: pl.BlockSpec((tm, w), lambda i: (i, 0))
    return pl.pallas_call(
        functools.partial(_merge_ffn_kernel, tm=tm, final_norm=final_norm),
        grid=(N // tm,),
        in_specs=[row(D), row(W_A), row(W_A), row(W_B), row(W_C),
                  _const_spec((N_A, CHUNK, CHUNK)), _const_spec((CHUNK, W_A)),
                  _const_spec((1, D)), _const_spec((D, D)),
                  _const_spec((1, D)), _const_spec((D, 2 * D_FF)), _const_spec((D_FF, D)), _const_spec((1, D))],
        out_specs=row(D),
        out_shape=jax.ShapeDtypeStruct((N, D), F32),
        compiler_params=_cparams("parallel"),
        name="merge_ffn",
    )(x, u, va, b_out, c_out, wmix, bmix, gmix.reshape(1, D), wo,
      gffn.reshape(1, D), win, wout, gfin.reshape(1, D))


def _split3(x):
    hi = x.astype(BF)
    r = x - hi.astype(F32)
    mid = r.astype(BF)
    lo = (r - mid.astype(F32)).astype(BF)
    return hi, mid, lo


def _fcum_kernel(lfr_ref, lfc_ref, fr_ref, fc_ref, *, S):
    ii = lax.broadcasted_iota(jnp.int32, (BLK, BLK), 0)
    jj = lax.broadcasted_iota(jnp.int32, (BLK, BLK), 1)
    t_incl = jnp.where(ii <= jj, 1.0, 0.0).astype(BF)
    l_incl = jnp.where(jj <= ii, 1.0, 0.0).astype(BF)
    rhs1 = jnp.concatenate([t_incl, jnp.ones((BLK, BLK), BF)], axis=1)
    rhs = jnp.concatenate([rhs1, rhs1, rhs1], axis=0)
    carry_r = jnp.zeros((SUBLANE, BLK), F32)
    carry_c = jnp.zeros((1, LANE), F32)
    for blk in range(S // BLK):
        sl = slice(blk * BLK, (blk + 1) * BLK)
        cs = _dot(jnp.concatenate(_split3(lfr_ref[:, sl]), axis=1), rhs)
        fr_ref[:, sl] = cs[:, :BLK] + carry_r
        carry_r = carry_r + cs[:, BLK:]
        hi, mid, lo = _split3(lfc_ref[sl, :])
        fc = _dot(l_incl, hi) + _dot(l_incl, mid) + _dot(l_incl, lo) + carry_c
        fc_ref[sl, :] = fc
        carry_c = fc[BLK - 1:BLK, :]


def _fcum(lf_row, lf_col, *, layer):
    _, B, _, S = lf_row.shape
    return pl.pallas_call(
        functools.partial(_fcum_kernel, S=S),
        grid=(B,),
        in_specs=[pl.BlockSpec((None, None, SUBLANE, S), lambda b: (layer, b, 0, 0)),
                  pl.BlockSpec((None, S, LANE), lambda b: (b, 0, 0))],
        out_specs=[pl.BlockSpec((None, SUBLANE, S), lambda b: (b, 0, 0)),
                   pl.BlockSpec((None, S, LANE), lambda b: (b, 0, 0))],
        out_shape=[jax.ShapeDtypeStruct((B, SUBLANE, S), F32), jax.ShapeDtypeStruct((B, S, LANE), F32)],
        compiler_params=_cparams("parallel"),
        name="fox_cumsum",
    )(lf_row, lf_col)


K_TILE_BLOCKS = 4


def _ktile(ref, h, start, width):
    return ref[h * HEAD_DIM:(h + 1) * HEAD_DIM, pl.ds(pl.multiple_of(start, BLK), width)]


def _fox_prompt_kernel(q_ref, kT_ref, vT_ref, fr_ref, fc_ref, o_ref, m_scr, acc_scr, fq_scr, *, kw):
    qi = pl.program_id(1)
    nblk = kw // BLK
    gd = qi // nblk
    causal = lax.broadcasted_iota(jnp.int32, (BLK, BLK), 1) <= lax.broadcasted_iota(jnp.int32, (BLK, BLK), 0)
    qs = [q_ref[:, h * HEAD_DIM:(h + 1) * HEAD_DIM].astype(BF) for h in range(N_C)]
    for h in range(N_C):
        fq_scr[h] = jnp.broadcast_to(fc_ref[:, h:h + 1], (BLK, LANE))

    def step(g, width, mask_last, first):
        ones_rows = jnp.ones((SUBLANE, width), BF)
        nb = width // BLK
        st = [dict() for _ in range(N_C)]
        for h, s in _issue_order(N_C, 2, FOX_SKEW):
            d = st[h]
            if s == 0:
                frow = fr_ref[h:h + 1, pl.ds(pl.multiple_of(g * kw, BLK), width)]
                qk = _dot(qs[h], _ktile(kT_ref, h, g * kw, width).astype(BF))
                fq = fq_scr[h]
                z = [qk[:, b * BLK:(b + 1) * BLK] + fq - frow[:, b * BLK:(b + 1) * BLK] for b in range(nb)]
                if mask_last is not None:
                    z[-1] = jnp.where(mask_last, z[-1], NEG)
                zmax = z[0]
                for zb in z[1:]:
                    zmax = jnp.maximum(zmax, zb)
                m_new = jnp.broadcast_to(jnp.max(zmax, axis=1, keepdims=True), (BLK, LANE))
                if not first:
                    m_old = m_scr[h]
                    m_new = jnp.maximum(m_old, m_new)
                    d["alpha"] = jnp.exp(m_old - m_new)[:, :HEAD_DIM + SUBLANE]
                p = [jnp.exp(zb - m_new).astype(BF) for zb in z]
                d["p"] = jnp.concatenate(p, axis=1) if nb > 1 else p[0]
                m_scr[h] = m_new
            else:
                v_ext = jnp.concatenate([_ktile(vT_ref, h, g * kw, width).astype(BF), ones_rows], axis=0)
                pv = _dot_nt(d["p"], v_ext)
                acc_scr[h] = pv if first else acc_scr[h] * d["alpha"] + pv
        return 0

    diag = [functools.partial(step, gd, (w + 1) * BLK, causal, True) for w in range(nblk)]
    if nblk > 1:
        lax.switch(qi % nblk, diag)
    else:
        diag[0]()
    lax.fori_loop(0, gd, lambda g, _: step(g, kw, None, False), 0)
    outs = []
    for h in range(N_C):
        acc = acc_scr[h]
        outs.append(acc[:, :HEAD_DIM] / acc[:, HEAD_DIM:HEAD_DIM + 1])
    o_ref[...] = jnp.concatenate(outs, axis=1)


def _fox_prompt(q, kT, vT, f_row, f_col, *, layer):
    B, S, W = q.shape
    kw = min(K_TILE_BLOCKS * BLK, S)
    return pl.pallas_call(
        functools.partial(_fox_prompt_kernel, kw=kw),
        grid=(B, S // BLK),
        in_specs=[pl.BlockSpec((None, BLK, W), lambda b, i: (b, i, 0)),
                  pl.BlockSpec((None, None, W, S), lambda b, i: (layer, b, 0, 0)),
                  pl.BlockSpec((None, None, W, S), lambda b, i: (layer, b, 0, 0)),
                  pl.BlockSpec((None, SUBLANE, S), lambda b, i: (b, 0, 0)),
                  pl.BlockSpec((None, BLK, LANE), lambda b, i: (b, i, 0))],
        out_specs=pl.BlockSpec((None, BLK, W), lambda b, i: (b, i, 0)),
        out_shape=jax.ShapeDtypeStruct((B, S, W), F32),
        scratch_shapes=[pltpu.VMEM((N_C, BLK, LANE), F32), pltpu.VMEM((N_C, BLK, HEAD_DIM + SUBLANE), F32),
                        pltpu.VMEM((N_C, BLK, LANE), F32)],
        compiler_params=_cparams("parallel", "arbitrary"),
        name="fox_prompt",
    )(q, kT, vT, f_row, f_col)


def _suffix_sum_rhs():
    k = lax.broadcasted_iota(jnp.int32, (2 * BLK, 2 * BLK), 0) % BLK
    n = lax.broadcasted_iota(jnp.int32, (2 * BLK, 2 * BLK), 1)
    return jnp.where((n >= BLK) | (k > n), 1.0, 0.0).astype(BF)


def _sb_pre(z, mask_last):
    nb = z.shape[1] // BLK
    sp = jnp.log(1.0 + jnp.exp(-jnp.abs(z)))
    ls = jnp.minimum(z, 0.0) - sp
    l1m = ls - z
    blk = lambda x, b: x[:, b * BLK:(b + 1) * BLK]
    lhs = []
    for b in range(nb):
        lb = blk(l1m, b)
        if mask_last is not None and b == nb - 1:
            lb = jnp.where(mask_last, lb, 0.0)
        hi = lb.astype(BF)
        lo = (lb - hi.astype(F32)).astype(BF)
        lhs.append(jnp.concatenate([hi, lo], axis=1))
    return ls, (jnp.concatenate(lhs, axis=0) if nb > 1 else lhs[0])


def _sb_post(ls, cs, c, mask_last):
    r, n = ls.shape
    nb = n // BLK
    a = [None] * nb
    for b in reversed(range(nb)):
        csb = cs[b * r:(b + 1) * r]
        e = ls[:, b * BLK:(b + 1) * BLK] + csb[:, :BLK]
        ab = jnp.exp(e if c is None else e + c)
        if mask_last is not None and b == nb - 1:
            ab = jnp.where(mask_last, ab, 0.0)
        a[b] = ab.astype(BF)
        c = csb[:, BLK:] if c is None else c + csb[:, BLK:]
    return (jnp.concatenate(a, axis=1) if nb > 1 else a[0]), c


def _issue_order(n_chains, n_stages, dist):
    items = [(h + s * dist, s, h) for h in range(n_chains) for s in range(n_stages)]
    return [(h, s) for _, s, h in sorted(items)]


SB_SKEW = 2
FOX_SKEW = 2


def _sb_prompt_kernel(q_ref, kT_ref, vT_ref, rhs_ref, o_ref, c_scr, acc_scr, *, kw):
    qi = pl.program_id(1)
    nblk = kw // BLK
    gd = qi // nblk
    strict = lax.broadcasted_iota(jnp.int32, (BLK, BLK), 1) < lax.broadcasted_iota(jnp.int32, (BLK, BLK), 0)
    qs = [q_ref[:, h * HEAD_DIM:(h + 1) * HEAD_DIM].astype(BF) for h in range(N_B)]

    def step(g, width, mask_last, first):
        st = [dict() for _ in range(N_B)]
        for h, s in _issue_order(N_B, 3, SB_SKEW):
            d = st[h]
            if s == 0:
                z = _dot(qs[h], _ktile(kT_ref, h, g * kw, width).astype(BF))
                d["ls"], d["lhs"] = _sb_pre(z, mask_last)
            elif s == 1:
                cs = _dot(d["lhs"], rhs_ref[...])
                d["a"], c = _sb_post(d["ls"], cs, None if first else c_scr[h], mask_last)
                c_scr[h] = c
            else:
                pv = _dot_nt(d["a"], _ktile(vT_ref, h, g * kw, width).astype(BF))
                acc_scr[h] = pv if first else acc_scr[h] + pv
        return 0

    diag = [functools.partial(step, gd, (w + 1) * BLK, strict, True) for w in range(nblk)]
    if nblk > 1:
        lax.switch(qi % nblk, diag)
    else:
        diag[0]()
    lax.fori_loop(0, gd, lambda i, _: step(gd - 1 - i, kw, None, False), 0)
    o_ref[...] = jnp.concatenate([acc_scr[h] for h in range(N_B)], axis=1)


def _sb_prompt(q, kT, vT, *, layer):
    B, S, W = q.shape
    kw = min(K_TILE_BLOCKS * BLK, S)
    return pl.pallas_call(
        functools.partial(_sb_prompt_kernel, kw=kw),
        grid=(B, S // BLK),
        in_specs=[pl.BlockSpec((None, BLK, W), lambda b, i: (b, i, 0)),
                  pl.BlockSpec((None, None, W, S), lambda b, i: (layer, b, 0, 0)),
                  pl.BlockSpec((None, None, W, S), lambda b, i: (layer, b, 0, 0)),
                  _const_spec((2 * BLK, 2 * BLK))],
        out_specs=pl.BlockSpec((None, BLK, W), lambda b, i: (b, i, 0)),
        out_shape=jax.ShapeDtypeStruct((B, S, W), F32),
        scratch_shapes=[pltpu.VMEM((N_B, BLK, BLK), F32), pltpu.VMEM((N_B, BLK, HEAD_DIM), F32)],
        compiler_params=_cparams("parallel", "arbitrary"),
        name="sb_prompt",
    )(q, kT, vT, _suffix_sum_rhs())


def _stack_heads(parts):
    return jnp.concatenate(parts, axis=0)


def _decode_kernel(pt_ref, qb_ref, qc_ref, kb_ref, vb_ref, kc_ref, vc_ref, lf_ref, *refs,
                   layer, n_pages, t_new, n_seq):
    n = n_pages
    (sbk_hbm, sbv_hbm, fk_hbm, fv_hbm, flf_hbm, ob_ref, oc_ref,
     sbk_buf, sbv_buf, fk_buf, fv_buf, flf_buf, sem) = refs
    b = pl.program_id(0)
    slot = lax.rem(b, 2)

    def page_copies(seq, dst_slot):
        cps = []
        for p in range(n):
            pg = pt_ref[seq, p]
            for src, dst in ((sbk_hbm, sbk_buf), (sbv_hbm, sbv_buf), (fk_hbm, fk_buf), (fv_hbm, fv_buf),
                             (flf_hbm, flf_buf)):
                cps.append(pltpu.make_async_copy(src.at[layer, pg], dst.at[dst_slot, p], sem.at[dst_slot]))
        return cps

    @pl.when(b == 0)
    def _():
        for cp in page_copies(0, 0):
            cp.start()

    nxt = jnp.minimum(b + 1, n_seq - 1)
    for cp in page_copies(nxt, 1 - slot):
        cp.start()
    for cp in page_copies(b, slot):
        cp.wait()

    sbk = [sbk_buf.at[slot, p] for p in range(n)]
    sbv = [sbv_buf.at[slot, p] for p in range(n)]
    fk = [fk_buf.at[slot, p] for p in range(n)]
    fv = [fv_buf.at[slot, p] for p in range(n)]
    flf = [flf_buf.at[slot, p] for p in range(n)]
    T = t_new
    rows = N_B * T
    hs = lambda x, h: x[:, h * HEAD_DIM:(h + 1) * HEAD_DIM]

    sb_rhs = _suffix_sum_rhs()
    kk = lax.broadcasted_iota(jnp.int32, (3 * PAGE, 2 * PAGE), 0) % PAGE
    nn = lax.broadcasted_iota(jnp.int32, (3 * PAGE, 2 * PAGE), 1)
    lf_rhs = jnp.where((nn >= PAGE) | (kk > nn), 1.0, 0.0).astype(BF)
    t_idx = lax.broadcasted_iota(jnp.int32, (rows, PAGE), 0) % T
    s_idx = lax.broadcasted_iota(jnp.int32, (rows, PAGE), 1)
    zpad = jnp.zeros((PAGE - T, HEAD_DIM), BF)
    ones_rows = jnp.ones((SUBLANE, PAGE), BF)

    page = lambda x, p: x[:, p * PAGE:(p + 1) * PAGE]

    qb = qb_ref[...].astype(BF)
    kb_new = kb_ref[...].astype(BF)
    vb_new = vb_ref[...].astype(BF)

    lf_new = lf_ref[...]
    g_rows = [lf_new[0:1]]
    for t in range(1, T):
        g_rows.append(g_rows[-1] + lf_new[t:t + 1])
    g_col = jnp.concatenate(g_rows, axis=0)
    eye = lax.broadcasted_iota(jnp.int32, (T, PAGE), 0) == lax.broadcasted_iota(jnp.int32, (T, PAGE), 1)
    g_q = _stack_heads([jnp.broadcast_to(g_col[:, h:h + 1], (T, PAGE)) for h in range(N_C)])
    g_k = _stack_heads([jnp.broadcast_to(
        jnp.sum(jnp.where(eye, jnp.broadcast_to(g_col[:, h:h + 1], (T, PAGE)), 0.0), axis=0, keepdims=True),
        (T, PAGE)) for h in range(N_C)])
    qc = qc_ref[...].astype(BF)
    kc_new = kc_ref[...].astype(BF)
    vc_new = vc_ref[...].astype(BF)

    zb_new = _stack_heads([_dot_nt(hs(qb, h), jnp.concatenate([hs(kb_new, h), zpad], axis=0)) for h in range(N_B)])
    zb = jnp.concatenate([_stack_heads([_dot(hs(qb, h), sbk[p][h].astype(BF)) for h in range(N_B)])
                          for p in range(n)], axis=1)

    lf_pad = jnp.zeros((SUBLANE - N_C, PAGE), F32)
    lf_all = jnp.concatenate([jnp.concatenate([flf[p][...], lf_pad], axis=0) for p in range(n)], axis=0)
    cs = _dot(jnp.concatenate(_split3(lf_all), axis=1), lf_rhs)
    zc_new = _stack_heads([_dot_nt(hs(qc, h), jnp.concatenate([hs(kc_new, h), zpad], axis=0)) for h in range(N_C)])
    zc_pages = [_stack_heads([_dot(hs(qc, h), fk[p][h].astype(BF)) for h in range(N_C)]) for p in range(n)]

    mask_new = s_idx < t_idx
    ls_new, lhs_new = _sb_pre(zb_new, mask_new)
    ls_all, lhs_all = _sb_pre(zb, None)
    cs_new = _dot(lhs_new, sb_rhs)
    cs_all = _dot(lhs_all, sb_rhs)

    zc_new = jnp.where(s_idx <= t_idx, zc_new + g_q - g_k, NEG)
    r_carry = jnp.zeros((SUBLANE, PAGE), F32)
    for p in reversed(range(n)):
        csp = cs[p * SUBLANE:(p + 1) * SUBLANE]
        r_page = csp[:, :PAGE] + r_carry
        r_carry = r_carry + csp[:, PAGE:]
        bias = _stack_heads([jnp.broadcast_to(r_page[h:h + 1], (T, PAGE)) for h in range(N_C)]) + g_q
        zc_pages[p] = zc_pages[p] + bias
    zc = jnp.concatenate(zc_pages, axis=1)
    m = jnp.maximum(jnp.max(zc_new, axis=1, keepdims=True), jnp.max(zc, axis=1, keepdims=True))
    p_new = jnp.exp(zc_new - m)
    p_all = jnp.exp(zc - m)
    den = jnp.sum(p_new, axis=1, keepdims=True) + jnp.sum(p_all, axis=1, keepdims=True)
    p_new = p_new.astype(BF)
    p_all = p_all.astype(BF)
    accs = [_dot(p_new[h * T:(h + 1) * T], jnp.concatenate([hs(vc_new, h), zpad], axis=0)) for h in range(N_C)]
    for p in range(n):
        accs = [accs[h] + _dot_nt(page(p_all, p)[h * T:(h + 1) * T], fv[p][h].astype(BF)) for h in range(N_C)]

    a_new, c = _sb_post(ls_new, cs_new, None, mask_new)
    a_all, _ = _sb_post(ls_all, cs_all, c, None)
    acc = [_dot(a_new[h * T:(h + 1) * T], jnp.concatenate([hs(vb_new, h), zpad], axis=0)) for h in range(N_B)]
    for p in range(n):
        acc = [acc[h] + _dot_nt(page(a_all, p)[h * T:(h + 1) * T], sbv[p][h].astype(BF)) for h in range(N_B)]

    out = _stack_heads(accs) / den
    oc_ref[...] = jnp.concatenate([out[h * T:(h + 1) * T] for h in range(N_C)], axis=1)
    ob_ref[...] = jnp.concatenate(acc, axis=1)

    @pl.when(b == n_seq - 1)
    def _():
        for cp in page_copies(nxt, 1 - slot):
            cp.wait()


def _decode(page_table, qb, qc, kb, vb, kc, vc, lf, sbkT, sbvT, fkT, fvT, flfT, *, layer, t_new):
    nb, n_pages = page_table.shape
    N = qb.shape[0]
    tok = lambda w: pl.BlockSpec((t_new, w), lambda b, pt: (b, 0))
    hbm = pl.BlockSpec(memory_space=pl.ANY)
    kv_buf = pltpu.VMEM((2, n_pages, N_B, HEAD_DIM, PAGE), F32)
    return pl.pallas_call(
        functools.partial(_decode_kernel, layer=layer, n_pages=n_pages, t_new=t_new, n_seq=nb),
        grid_spec=pltpu.PrefetchScalarGridSpec(
            num_scalar_prefetch=1,
            grid=(nb,),
            in_specs=[tok(W_B), tok(W_C), tok(W_B), tok(W_B), tok(W_C), tok(W_C), tok(LANE)] + [hbm] * 5,
            out_specs=[tok(W_B), tok(W_C)],
            scratch_shapes=[kv_buf, kv_buf, kv_buf, kv_buf, pltpu.VMEM((2, n_pages, N_C, PAGE), F32),
                            pltpu.SemaphoreType.DMA((2,))],
        ),
        out_shape=[jax.ShapeDtypeStruct((N, W_B), F32), jax.ShapeDtypeStruct((N, W_C), F32)],
        compiler_params=_cparams("arbitrary"),
        name="decode_attention",
    )(page_table, qb, qc, kb, vb, kc, vc, lf, sbkT, sbvT, fkT, fvT, flfT)


TM_PROMPT = 512
TM_SAMPLE = 256


def kernel(x_prompt, x_sample, cache_sb_k, cache_sb_v, cache_fox_k, cache_fox_v, cache_fox_logf, page_table, g_attn, w_in, b_f, g_v, w_s, b_s, g_mix, w_o, g_ffn, w_ffn_in, w_ffn_out, g_final):
    B, S, D = x_prompt.shape
    DB, T, _ = x_sample.shape
    depth = g_attn.shape[0]
    assert D == D_MODEL and S % TM_PROMPT == 0 and (DB * T) % TM_SAMPLE == 0
    assert CHUNK % T == 0 and TM_SAMPLE % CHUNK == 0 and cache_sb_k.shape[2] == PAGE

    kv_t = lambda c: jnp.transpose(c, (0, 1, 3, 4, 2))
    sbkT, sbvT, fkT, fvT = kv_t(cache_sb_k), kv_t(cache_sb_v), kv_t(cache_fox_k), kv_t(cache_fox_v)
    flfT = jnp.transpose(cache_fox_logf, (0, 1, 3, 2))

    yp = x_prompt
    ys = x_sample.reshape(DB * T, D)
    stacked = None
    outs_s = [[] for _ in range(6)]
    for l in range(depth):
        last = l == depth - 1
        w_tokT, w_kvT, bf_col, bf_row = _split_w_in(w_in[l], b_f[l])
        wo, wfi, wfo = w_o[l].astype(BF), w_ffn_in[l].astype(BF), w_ffn_out[l].astype(BF)

        u, va, qb, qc, lfc, *stacked = _inproj_prompt(
            yp, g_attn[l], w_tokT, w_kvT, bf_col, bf_row, g_v[l], stacked, layer=l, depth=depth, tm=TM_PROMPT)
        kbT, vbT, kcT, vcT, lfr = stacked
        f_row, f_col = _fcum(lfr, lfc, layer=l)
        c_out = _fox_prompt(qc, kcT, vcT, f_row, f_col, layer=l)
        b_out = _sb_prompt(qb, kbT, vbT, layer=l)
        wmix, bmix = _mix_operands_prompt(w_s[l], b_s[l])
        flat = lambda a: a.reshape(B * S, a.shape[-1])
        yp = _merge_ffn(flat(yp), flat(u), flat(va), flat(b_out), flat(c_out), wmix, bmix, g_mix[l], wo,
                        g_ffn[l], wfi, wfo, g_final, tm=TM_PROMPT, final_norm=last).reshape(B, S, D)

        u, va, qb, qc, lfc, kb, vb, kc, vc = _inproj_sample(
            ys, g_attn[l], w_tokT, w_kvT, bf_col, g_v[l], tm=TM_SAMPLE)
        b_out, c_out = _decode(page_table, qb, qc, kb, vb, kc, vc, lfc, sbkT, sbvT, fkT, fvT, flfT,
                               layer=l, t_new=T)
        wmix, bmix = _mix_operands_sample(w_s[l], b_s[l], T)
        ys = _merge_ffn(ys, u, va, b_out, c_out, wmix, bmix, g_mix[l], wo,
                        g_ffn[l], wfi, wfo, g_final, tm=TM_SAMPLE, final_norm=last)
        for dst, a in zip(outs_s, (kb, vb, kc, vc)):
            dst.append(a.reshape(DB, T, N_B, HEAD_DIM))
        outs_s[4].append(lfc[:, :N_C].reshape(DB, T, N_C))
        outs_s[5].append(va.reshape(DB, T, W_A))

    to_seq_major = lambda a: jnp.transpose(a.reshape(depth, B, -1, HEAD_DIM, S), (0, 1, 4, 2, 3))
    kbT, vbT, kcT, vcT, lfr = stacked
    p_lf = jnp.transpose(lfr[:, :, :N_C, :], (0, 1, 3, 2))
    return (yp, ys.reshape(DB, T, D),
            to_seq_major(kbT), to_seq_major(vbT), to_seq_major(kcT), to_seq_major(vcT), p_lf,
            jnp.stack(outs_s[0]), jnp.stack(outs_s[1]), jnp.stack(outs_s[2]), jnp.stack(outs_s[3]),
            jnp.stack(outs_s[4]), jnp.stack(outs_s[5]))
```

```python
import functools
import math

import jax
import jax.numpy as jnp
import numpy as np
from jax import lax
from jax.experimental import pallas as pl
from jax.experimental.pallas import tpu as pltpu

D_MODEL = 1024
HEAD_DIM = 64
N_A, N_B, N_C = 4, 6, 6
W_A, W_B, W_C = N_A * HEAD_DIM, N_B * HEAD_DIM, N_C * HEAD_DIM
CHUNK = 128
PAGE = 128
D_FF = 2816
EPS = 1e-6
NEG = -1e30
LOG2E = 1.4426950408889634
LANE = 128
SUBLANE = 8
BLK = 128
VMEM_LIMIT = 48 * 1024 * 1024
BF = jnp.bfloat16
F32 = jnp.float32

_O_U, _O_VA, _O_QB, _O_KB, _O_VB = 0, W_A, 2 * W_A, 2 * W_A + W_B, 2 * W_A + 2 * W_B
_O_QC = 2 * W_A + 3 * W_B
_O_KC, _O_VC, _O_F = _O_QC + W_C, _O_QC + 2 * W_C, _O_QC + 3 * W_C


def _cparams(*sem):
    return pltpu.CompilerParams(dimension_semantics=sem, vmem_limit_bytes=VMEM_LIMIT)


def _gelu(x):
    return 0.5 * x * (1.0 + jnp.tanh(math.sqrt(2.0 / math.pi) * (x + 0.044715 * (x * x * x))))


def _log_sigmoid(x):
    return jnp.minimum(x, 0.0) - jnp.log(1.0 + jnp.exp(-jnp.abs(x)))


def _rms(x, g):
    return x * lax.rsqrt(jnp.mean(x * x, axis=-1, keepdims=True) + EPS) * g


def _dot(a, b):
    return jnp.dot(a, b, preferred_element_type=F32)


def _dot_nt(a, b):
    return lax.dot_general(a, b, (((1,), (1,)), ((), ())), preferred_element_type=F32)


def _inproj_tok_epilogue(zt, bf_col, gv, u_ref, va_ref, qb_ref, qc_ref, lfc_ref, qc_scale):
    u_ref[...] = _gelu(zt[:, 0:W_A])
    va_ref[...] = _rms(_gelu(zt[:, W_A:2 * W_A]), gv)
    qb_ref[...] = (zt[:, 2 * W_A:2 * W_A + W_B] * (LOG2E * HEAD_DIM ** -0.5)).astype(qb_ref.dtype)
    qc_ref[...] = (zt[:, 2 * W_A + W_B:2 * W_A + W_B + W_C] * qc_scale).astype(qc_ref.dtype)
    o = 2 * W_A + W_B + W_C
    lfc_ref[...] = _log_sigmoid(zt[:, o:o + LANE] + bf_col)


def _inproj_prompt_kernel(x_ref, g_ref, wt_ref, wT_ref, bfc_ref, bfr_ref, gv_ref, *refs, layer, init_stack):
    if not init_stack:
        refs = refs[5:]
    u_ref, va_ref, qb_ref, qc_ref, lfc_ref, kbT_ref, vbT_ref, kcT_ref, vcT_ref, lfr_ref = refs

    def put(ref, val):
        if init_stack:
            for d in range(ref.shape[0]):
                ref[d] = val if d == layer else jnp.zeros_like(val)
        else:
            ref[...] = val

    xn = _rms(x_ref[...], g_ref[...]).astype(BF)
    zt = _dot_nt(xn, wt_ref[...])
    _inproj_tok_epilogue(zt, bfc_ref[...], gv_ref[...], u_ref, va_ref, qb_ref, qc_ref, lfc_ref,
                         LOG2E * HEAD_DIM ** -0.5)
    zT = _dot_nt(wT_ref[...], xn)
    put(kbT_ref, zT[0:W_B])
    put(vbT_ref, zT[W_B:2 * W_B])
    put(kcT_ref, zT[2 * W_B:2 * W_B + W_C])
    put(vcT_ref, zT[2 * W_B + W_C:2 * W_B + 2 * W_C])
    o = 2 * W_B + 2 * W_C
    put(lfr_ref, _log_sigmoid(zT[o:o + SUBLANE] + bfr_ref[...]))


def _inproj_sample_kernel(x_ref, g_ref, wt_ref, wkv_ref, bfc_ref, gv_ref,
                          u_ref, va_ref, qb_ref, qc_ref, lfc_ref,
                          kb_ref, vb_ref, kc_ref, vc_ref):
    xn = _rms(x_ref[...], g_ref[...]).astype(BF)
    zt = _dot_nt(xn, wt_ref[...])
    _inproj_tok_epilogue(zt, bfc_ref[...], gv_ref[...], u_ref, va_ref, qb_ref, qc_ref, lfc_ref, HEAD_DIM ** -0.5)
    zk = _dot_nt(xn, wkv_ref[0:2 * W_B + 2 * W_C, :])
    kb_ref[...] = zk[:, 0:W_B]
    vb_ref[...] = zk[:, W_B:2 * W_B]
    kc_ref[...] = zk[:, 2 * W_B:2 * W_B + W_C]
    vc_ref[...] = zk[:, 2 * W_B + W_C:2 * W_B + 2 * W_C]


def _split_w_in(w_in, b_f):
    depth = w_in.shape[0]
    wT = jnp.swapaxes(w_in, 1, 2)
    wfT = wT[:, _O_F:]
    pad_rows = lambda a, n: jnp.pad(a, ((0, 0), (0, n - a.shape[1]), (0, 0)))
    w_tokT = jnp.concatenate([wT[:, _O_U:_O_QB], wT[:, _O_QB:_O_KB], wT[:, _O_QC:_O_KC],
                              pad_rows(wfT, LANE)], axis=1).astype(BF)
    w_kvT = jnp.concatenate([wT[:, _O_KB:_O_QC], wT[:, _O_KC:_O_F], pad_rows(wfT, SUBLANE)], axis=1).astype(BF)
    bf_col = jnp.pad(b_f, ((0, 0), (0, LANE - N_C))).reshape(depth, 1, LANE)
    bf_row = jnp.pad(b_f, ((0, 0), (0, SUBLANE - N_C))).reshape(depth, SUBLANE, 1)
    return w_tokT, w_kvT, bf_col, bf_row


def _const_spec(shape):
    nd = len(shape)
    return pl.BlockSpec(shape, lambda *_: (0,) * nd, pipeline_mode=pl.Buffered(1))


def _layer_spec(arr, layer):
    nd = arr.ndim - 1
    return pl.BlockSpec((None,) + arr.shape[1:], lambda *_: (layer,) + (0,) * nd, pipeline_mode=pl.Buffered(1))


def _inproj_prompt(x, g, w_tokT, w_kvT, bf_col, bf_row, gv, stacked, *, layer, tm):
    B, S, D = x.shape
    depth = g.shape[0]
    init_stack = stacked is None
    row = lambda w: pl.BlockSpec((None, tm, w), lambda b, s: (b, s, 0))
    if init_stack:
        colT = lambda h: pl.BlockSpec((depth, None, h, tm), lambda b, s: (0, b, 0, s))
    else:
        colT = lambda h: pl.BlockSpec((None, None, h, tm), lambda b, s: (layer, b, 0, s))
    f = lambda *sh: jax.ShapeDtypeStruct(sh, F32)
    n_in = 7
    return pl.pallas_call(
        functools.partial(_inproj_prompt_kernel, layer=layer, init_stack=init_stack),
        grid=(B, S // tm),
        in_specs=[row(D)] + [_layer_spec(a, layer) for a in (g, w_tokT, w_kvT, bf_col, bf_row, gv)]
                 + ([] if init_stack else [pl.BlockSpec(memory_space=pl.ANY)] * 5),
        out_specs=[row(W_A), row(W_A), row(W_B), row(W_C), row(LANE),
                   colT(W_B), colT(W_B), colT(W_C), colT(W_C), colT(SUBLANE)],
        out_shape=[f(B, S, W_A), f(B, S, W_A), jax.ShapeDtypeStruct((B, S, W_B), BF),
                   jax.ShapeDtypeStruct((B, S, W_C), BF), f(B, S, LANE),
                   f(depth, B, W_B, S), f(depth, B, W_B, S), f(depth, B, W_C, S), f(depth, B, W_C, S),
                   f(depth, B, SUBLANE, S)],
        input_output_aliases={} if init_stack else {n_in + j: 5 + j for j in range(5)},
        compiler_params=_cparams("parallel", "parallel"),
        name="inproj_prompt",
    )(x, g, w_tokT, w_kvT, bf_col, bf_row, gv, *([] if init_stack else stacked))


def _inproj_sample(x, g, w_tokT, w_kvT, bf_col, gv, *, layer, tm):
    N, D = x.shape
    row = lambda w: pl.BlockSpec((tm, w), lambda i: (i, 0))
    f = lambda *sh: jax.ShapeDtypeStruct(sh, F32)
    return pl.pallas_call(
        _inproj_sample_kernel,
        grid=(N // tm,),
        in_specs=[row(D)] + [_layer_spec(a, layer) for a in (g, w_tokT, w_kvT, bf_col, gv)],
        out_specs=[row(W_A), row(W_A), row(W_B), row(W_C), row(LANE),
                   row(W_B), row(W_B), row(W_C), row(W_C)],
        out_shape=[f(N, W_A), f(N, W_A), f(N, W_B), f(N, W_C), f(N, LANE),
                   f(N, W_B), f(N, W_B), f(N, W_C), f(N, W_C)],
        compiler_params=_cparams("parallel"),
        name="inproj_sample",
    )(x, g, w_tokT, w_kvT, bf_col, gv)


FF_CHUNK = 256


def _merge_ffn_kernel(x_ref, u_ref, va_ref, b_ref, c_ref, wmix_ref, bmix_ref, gmix_ref, wo_ref,
                      gffn_ref, win_ref, wout_ref, gfin_ref, y_ref, *, tm, final_norm):
    lane_grp = lax.broadcasted_iota(jnp.int32, (CHUNK, W_A), 1) // HEAD_DIM
    a_rows = []
    for r in range(tm // CHUNK):
        va = va_ref[r * CHUNK:(r + 1) * CHUNK, :]
        mixed = bmix_ref[...]
        for g in range(N_A):
            vg = jnp.where(lane_grp == g, va, 0.0).astype(BF)
            mixed = mixed + _dot(wmix_ref[g], vg)
        a_rows.append(u_ref[r * CHUNK:(r + 1) * CHUNK, :] * mixed)
    a_out = jnp.concatenate(a_rows, axis=0) if len(a_rows) > 1 else a_rows[0]
    gm = gmix_ref[...]
    cat = jnp.concatenate([_rms(a_out, gm[:, 0:W_A]).astype(BF),
                           _rms(b_ref[...], gm[:, W_A:W_A + W_B]).astype(BF),
                           _rms(c_ref[...], gm[:, W_A + W_B:]).astype(BF)], axis=1)
    y = x_ref[...] + _dot(cat, wo_ref[...])
    yn = _rms(y, gffn_ref[...]).astype(BF)
    acc = y
    for c in range(D_FF // FF_CHUNK):
        h = _dot(yn, win_ref[:, c * FF_CHUNK:(c + 1) * FF_CHUNK])
        gate = _dot(yn, win_ref[:, D_FF + c * FF_CHUNK:D_FF + (c + 1) * FF_CHUNK])
        act = (h * jax.nn.sigmoid(h) * gate).astype(BF)
        acc = acc + _dot(act, wout_ref[c * FF_CHUNK:(c + 1) * FF_CHUNK, :])
    if final_norm:
        acc = _rms(acc, gfin_ref[...])
    y_ref[...] = acc


def _mix_operands_prompt(w_s, b_s):
    wmix = jnp.tril(w_s).astype(BF)
    bmix = jnp.repeat(jnp.swapaxes(b_s, 1, 2), HEAD_DIM, axis=2)
    return wmix, bmix


def _mix_operands_sample(w_s, b_s, t):
    depth = w_s.shape[0]
    reps = CHUNK // t
    small = jnp.tril(w_s[:, :, :t, :t])
    eye = jnp.eye(reps, dtype=small.dtype)
    wmix = jnp.einsum("rs,dgij->dgrisj", eye, small).reshape(depth, N_A, CHUNK, CHUNK).astype(BF)
    bmix = jnp.tile(jnp.repeat(jnp.swapaxes(b_s[:, :, :t], 1, 2), HEAD_DIM, axis=2), (1, reps, 1))
    return wmix, bmix


def _merge_ffn(x, u, va, b_out, c_out, wmix, bmix, gmix, wo, gffn, win, wout, gfin, *, layer, tm, final_norm):
    N, D = x.shape
    row = lambda w: pl.BlockSpec((tm, w), lambda i: (i, 0))
    params = (wmix, bmix, gmix, wo, gffn, win, wout)
    return pl.pallas_call(
        functools.partial(_merge_ffn_kernel, tm=tm, final_norm=final_norm),
        grid=(N // tm,),
        in_specs=[row(D), row(W_A), row(W_A), row(W_B), row(W_C)]
                 + [_layer_spec(a, layer) for a in params] + [_const_spec((1, D))],
        out_specs=row(D),
        out_shape=jax.ShapeDtypeStruct((N, D), F32),
        compiler_params=_cparams("parallel"),
        name="merge_ffn",
    )(x, u, va, b_out, c_out, *params, gfin)


def _split3(x):
    hi = x.astype(BF)
    r = x - hi.astype(F32)
    mid = r.astype(BF)
    lo = (r - mid.astype(F32)).astype(BF)
    return hi, mid, lo


def _fcum_kernel(lfr_ref, lfc_ref, fr_ref, fc_ref, *, S):
    ii = lax.broadcasted_iota(jnp.int32, (BLK, BLK), 0)
    jj = lax.broadcasted_iota(jnp.int32, (BLK, BLK), 1)
    t_incl = jnp.where(ii <= jj, 1.0, 0.0).astype(BF)
    l_incl = jnp.where(jj <= ii, 1.0, 0.0).astype(BF)
    rhs1 = jnp.concatenate([t_incl, jnp.ones((BLK, BLK), BF)], axis=1)
    rhs = jnp.concatenate([rhs1, rhs1, rhs1], axis=0)
    carry_r = jnp.zeros((SUBLANE, BLK), F32)
    carry_c = jnp.zeros((1, LANE), F32)
    for blk in range(S // BLK):
        sl = slice(blk * BLK, (blk + 1) * BLK)
        cs = _dot(jnp.concatenate(_split3(lfr_ref[:, sl]), axis=1), rhs)
        fr_ref[:, sl] = (cs[:, :BLK] + carry_r) * LOG2E
        carry_r = carry_r + cs[:, BLK:]
        hi, mid, lo = _split3(lfc_ref[sl, :])
        fc = _dot(l_incl, hi) + _dot(l_incl, mid) + _dot(l_incl, lo) + carry_c
        fc_ref[sl, :] = fc * LOG2E
        carry_c = fc[BLK - 1:BLK, :]


def _fcum(lf_row, lf_col, *, layer):
    _, B, _, S = lf_row.shape
    return pl.pallas_call(
        functools.partial(_fcum_kernel, S=S),
        grid=(B,),
        in_specs=[pl.BlockSpec((None, None, SUBLANE, S), lambda b: (layer, b, 0, 0)),
                  pl.BlockSpec((None, S, LANE), lambda b: (b, 0, 0))],
        out_specs=[pl.BlockSpec((None, SUBLANE, S), lambda b: (b, 0, 0)),
                   pl.BlockSpec((None, S, LANE), lambda b: (b, 0, 0))],
        out_shape=[jax.ShapeDtypeStruct((B, SUBLANE, S), F32), jax.ShapeDtypeStruct((B, S, LANE), F32)],
        compiler_params=_cparams("parallel"),
        name="fox_cumsum",
    )(lf_row, lf_col)


K_TILE_BLOCKS = 4
FOX_K_TILE_BLOCKS = 8


def _ktile(ref, h, start, width):
    return ref[h * HEAD_DIM:(h + 1) * HEAD_DIM, pl.ds(pl.multiple_of(start, BLK), width)]


def _fox_prompt_kernel(q_ref, kT_ref, vT_ref, fr_ref, fc_ref, o_ref, m_scr, acc_scr, fq_scr, *, kw):
    qi = pl.program_id(1)
    nblk = kw // BLK
    gd = qi // nblk
    causal = lax.broadcasted_iota(jnp.int32, (BLK, BLK), 1) <= lax.broadcasted_iota(jnp.int32, (BLK, BLK), 0)
    qs = [q_ref[:, h * HEAD_DIM:(h + 1) * HEAD_DIM].astype(BF) for h in range(N_C)]
    for h in range(N_C):
        fq_scr[h] = jnp.broadcast_to(fc_ref[:, h:h + 1], (BLK, LANE))

    def step(g, width, mask_last, first):
        ones_rows = jnp.ones((SUBLANE, width), BF)
        nb = width // BLK
        st = [dict() for _ in range(N_C)]
        for h, s in _issue_order(N_C, 2, FOX_SKEW):
            d = st[h]
            if s == 0:
                frow = fr_ref[h:h + 1, pl.ds(pl.multiple_of(g * kw, BLK), width)]
                qk = _dot(qs[h], _ktile(kT_ref, h, g * kw, width).astype(BF))
                fq = fq_scr[h]
                z = [qk[:, b * BLK:(b + 1) * BLK] + fq - frow[:, b * BLK:(b + 1) * BLK] for b in range(nb)]
                if mask_last is not None:
                    z[-1] = jnp.where(mask_last, z[-1], NEG)
                zmax = z[0]
                for zb in z[1:]:
                    zmax = jnp.maximum(zmax, zb)
                m_new = jnp.broadcast_to(jnp.max(zmax, axis=1, keepdims=True), (BLK, LANE))
                if not first:
                    m_old = m_scr[h]
                    m_new = jnp.maximum(m_old, m_new)
                    d["alpha"] = jnp.exp2(m_old - m_new)[:, :HEAD_DIM + SUBLANE]
                p = [jnp.exp2(zb - m_new).astype(BF) for zb in z]
                d["p"] = jnp.concatenate(p, axis=1) if nb > 1 else p[0]
                m_scr[h] = m_new
            else:
                v_ext = jnp.concatenate([_ktile(vT_ref, h, g * kw, width).astype(BF), ones_rows], axis=0)
                pv = _dot_nt(d["p"], v_ext)
                acc_scr[h] = pv if first else acc_scr[h] * d["alpha"] + pv
        return 0

    diag = [functools.partial(step, gd, (w + 1) * BLK, causal, True) for w in range(nblk)]
    if nblk > 1:
        lax.switch(qi % nblk, diag)
    else:
        diag[0]()
    lax.fori_loop(0, gd, lambda g, _: step(g, kw, None, False), 0)
    outs = []
    for h in range(N_C):
        acc = acc_scr[h]
        outs.append(acc[:, :HEAD_DIM] / acc[:, HEAD_DIM:HEAD_DIM + 1])
    o_ref[...] = jnp.concatenate(outs, axis=1)


def _fox_prompt(q, kT, vT, f_row, f_col, *, layer):
    B, S, W = q.shape
    kw = min(FOX_K_TILE_BLOCKS * BLK, S)
    return pl.pallas_call(
        functools.partial(_fox_prompt_kernel, kw=kw),
        grid=(B, S // BLK),
        in_specs=[pl.BlockSpec((None, BLK, W), lambda b, i: (b, i, 0)),
                  pl.BlockSpec((None, None, W, S), lambda b, i: (layer, b, 0, 0)),
                  pl.BlockSpec((None, None, W, S), lambda b, i: (layer, b, 0, 0)),
                  pl.BlockSpec((None, SUBLANE, S), lambda b, i: (b, 0, 0)),
                  pl.BlockSpec((None, BLK, LANE), lambda b, i: (b, i, 0))],
        out_specs=pl.BlockSpec((None, BLK, W), lambda b, i: (b, i, 0)),
        out_shape=jax.ShapeDtypeStruct((B, S, W), F32),
        scratch_shapes=[pltpu.VMEM((N_C, BLK, LANE), F32), pltpu.VMEM((N_C, BLK, HEAD_DIM + SUBLANE), F32),
                        pltpu.VMEM((N_C, BLK, LANE), F32)],
        compiler_params=_cparams("parallel", "arbitrary"),
        name="fox_prompt",
    )(q, kT, vT, f_row, f_col)


def _suffix_sum_rhs():
    k = lax.broadcasted_iota(jnp.int32, (2 * BLK, 2 * BLK), 0) % BLK
    n = lax.broadcasted_iota(jnp.int32, (2 * BLK, 2 * BLK), 1)
    return jnp.where((n >= BLK) | (k > n), 1.0, 0.0).astype(BF)


def _sb_pre(z, mask_last):
    nb = z.shape[1] // BLK
    sp = jnp.log(1.0 + jnp.exp2(-jnp.abs(z))) * LOG2E
    ls = jnp.minimum(z, 0.0) - sp
    l1m = ls - z
    blk = lambda x, b: x[:, b * BLK:(b + 1) * BLK]
    lhs = []
    for b in range(nb):
        lb = blk(l1m, b)
        if mask_last is not None and b == nb - 1:
            lb = jnp.where(mask_last, lb, 0.0)
        hi = lb.astype(BF)
        lo = (lb - hi.astype(F32)).astype(BF)
        lhs.append(jnp.concatenate([hi, lo], axis=1))
    return ls, (jnp.concatenate(lhs, axis=0) if nb > 1 else lhs[0])


def _sb_post(ls, cs, c, mask_last):
    r, n = ls.shape
    nb = n // BLK
    a = [None] * nb
    for b in reversed(range(nb)):
        csb = cs[b * r:(b + 1) * r]
        e = ls[:, b * BLK:(b + 1) * BLK] + csb[:, :BLK]
        ab = jnp.exp2(e if c is None else e + c)
        if mask_last is not None and b == nb - 1:
            ab = jnp.where(mask_last, ab, 0.0)
        a[b] = ab.astype(BF)
        c = csb[:, BLK:] if c is None else c + csb[:, BLK:]
    return (jnp.concatenate(a, axis=1) if nb > 1 else a[0]), c


def _issue_order(n_chains, n_stages, dist):
    items = [(h + s * dist, s, h) for h in range(n_chains) for s in range(n_stages)]
    return [(h, s) for _, s, h in sorted(items)]


SB_SKEW = 4
FOX_SKEW = 6


def _sb_prompt_kernel(q_ref, kT_ref, vT_ref, rhs_ref, o_ref, c_scr, acc_scr, *, kw):
    qi = pl.program_id(1)
    nblk = kw // BLK
    gd = qi // nblk
    strict = lax.broadcasted_iota(jnp.int32, (BLK, BLK), 1) < lax.broadcasted_iota(jnp.int32, (BLK, BLK), 0)
    qs = [q_ref[:, h * HEAD_DIM:(h + 1) * HEAD_DIM].astype(BF) for h in range(N_B)]

    def step(g, width, mask_last, first):
        st = [dict() for _ in range(N_B)]
        for h, s in _issue_order(N_B, 3, SB_SKEW):
            d = st[h]
            if s == 0:
                z = _dot(qs[h], _ktile(kT_ref, h, g * kw, width).astype(BF))
                d["ls"], d["lhs"] = _sb_pre(z, mask_last)
            elif s == 1:
                cs = _dot(d["lhs"], rhs_ref[...])
                d["a"], c = _sb_post(d["ls"], cs, None if first else c_scr[h], mask_last)
                c_scr[h] = c
            else:
                pv = _dot_nt(d["a"], _ktile(vT_ref, h, g * kw, width).astype(BF))
                acc_scr[h] = pv if first else acc_scr[h] + pv
        return 0

    diag = [functools.partial(step, gd, (w + 1) * BLK, strict, True) for w in range(nblk)]
    if nblk > 1:
        lax.switch(qi % nblk, diag)
    else:
        diag[0]()
    lax.fori_loop(0, gd, lambda i, _: step(gd - 1 - i, kw, None, False), 0)
    o_ref[...] = jnp.concatenate([acc_scr[h] for h in range(N_B)], axis=1)


def _sb_prompt(q, kT, vT, *, layer):
    B, S, W = q.shape
    kw = min(K_TILE_BLOCKS * BLK, S)
    return pl.pallas_call(
        functools.partial(_sb_prompt_kernel, kw=kw),
        grid=(B, S // BLK),
        in_specs=[pl.BlockSpec((None, BLK, W), lambda b, i: (b, i, 0)),
                  pl.BlockSpec((None, None, W, S), lambda b, i: (layer, b, 0, 0)),
                  pl.BlockSpec((None, None, W, S), lambda b, i: (layer, b, 0, 0)),
                  _const_spec((2 * BLK, 2 * BLK))],
        out_specs=pl.BlockSpec((None, BLK, W), lambda b, i: (b, i, 0)),
        out_shape=jax.ShapeDtypeStruct((B, S, W), F32),
        scratch_shapes=[pltpu.VMEM((N_B, BLK, BLK), F32), pltpu.VMEM((N_B, BLK, HEAD_DIM), F32)],
        compiler_params=_cparams("parallel", "arbitrary"),
        name="sb_prompt",
    )(q, kT, vT, _suffix_sum_rhs())


def _stack_heads(parts):
    return jnp.concatenate(parts, axis=0)


def _decode_kernel(pt_ref, qb_ref, qc_ref, kb_ref, vb_ref, kc_ref, vc_ref, lf_ref, *refs,
                   layer, n_pages, t_new, n_seq):
    n = n_pages
    (sbk_hbm, sbv_hbm, fk_hbm, fv_hbm, flf_hbm, ob_ref, oc_ref,
     sbk_buf, sbv_buf, fk_buf, fv_buf, flf_buf, sem) = refs
    b = pl.program_id(0)
    slot = lax.rem(b, 2)

    def page_copies(seq, dst_slot):
        cps = []
        for p in range(n):
            pg = pt_ref[seq, p]
            for src, dst in ((sbk_hbm, sbk_buf), (sbv_hbm, sbv_buf), (fk_hbm, fk_buf), (fv_hbm, fv_buf),
                             (flf_hbm, flf_buf)):
                cps.append(pltpu.make_async_copy(src.at[layer, pg], dst.at[dst_slot, p], sem.at[dst_slot]))
        return cps

    @pl.when(b == 0)
    def _():
        for cp in page_copies(0, 0):
            cp.start()

    nxt = jnp.minimum(b + 1, n_seq - 1)
    for cp in page_copies(nxt, 1 - slot):
        cp.start()
    for cp in page_copies(b, slot):
        cp.wait()

    sbk = [sbk_buf.at[slot, p] for p in range(n)]
    sbv = [sbv_buf.at[slot, p] for p in range(n)]
    fk = [fk_buf.at[slot, p] for p in range(n)]
    fv = [fv_buf.at[slot, p] for p in range(n)]
    flf = [flf_buf.at[slot, p] for p in range(n)]
    T = t_new
    rows = N_B * T
    hs = lambda x, h: x[:, h * HEAD_DIM:(h + 1) * HEAD_DIM]

    sb_rhs = _suffix_sum_rhs()
    kk = lax.broadcasted_iota(jnp.int32, (3 * PAGE, 2 * PAGE), 0) % PAGE
    nn = lax.broadcasted_iota(jnp.int32, (3 * PAGE, 2 * PAGE), 1)
    lf_rhs = jnp.where((nn >= PAGE) | (kk > nn), 1.0, 0.0).astype(BF)
    t_idx = lax.broadcasted_iota(jnp.int32, (rows, PAGE), 0) % T
    s_idx = lax.broadcasted_iota(jnp.int32, (rows, PAGE), 1)
    zpad = jnp.zeros((PAGE - T, HEAD_DIM), BF)
    ones_rows = jnp.ones((SUBLANE, PAGE), BF)

    page = lambda x, p: x[:, p * PAGE:(p + 1) * PAGE]

    qb = qb_ref[...].astype(BF)
    kb_new = kb_ref[...].astype(BF)
    vb_new = vb_ref[...].astype(BF)

    lf_new = lf_ref[...]
    g_rows = [lf_new[0:1]]
    for t in range(1, T):
        g_rows.append(g_rows[-1] + lf_new[t:t + 1])
    g_col = jnp.concatenate(g_rows, axis=0)
    eye = lax.broadcasted_iota(jnp.int32, (T, PAGE), 0) == lax.broadcasted_iota(jnp.int32, (T, PAGE), 1)
    g_q = _stack_heads([jnp.broadcast_to(g_col[:, h:h + 1], (T, PAGE)) for h in range(N_C)])
    g_k = _stack_heads([jnp.broadcast_to(
        jnp.sum(jnp.where(eye, jnp.broadcast_to(g_col[:, h:h + 1], (T, PAGE)), 0.0), axis=0, keepdims=True),
        (T, PAGE)) for h in range(N_C)])
    qc = qc_ref[...].astype(BF)
    kc_new = kc_ref[...].astype(BF)
    vc_new = vc_ref[...].astype(BF)

    zb_new = _stack_heads([_dot_nt(hs(qb, h), jnp.concatenate([hs(kb_new, h), zpad], axis=0)) for h in range(N_B)])
    zb = jnp.concatenate([_stack_heads([_dot(hs(qb, h), sbk[p][h].astype(BF)) for h in range(N_B)])
                          for p in range(n)], axis=1)

    lf_pad = jnp.zeros((SUBLANE - N_C, PAGE), F32)
    lf_all = jnp.concatenate([jnp.concatenate([flf[p][...], lf_pad], axis=0) for p in range(n)], axis=0)
    cs = _dot(jnp.concatenate(_split3(lf_all), axis=1), lf_rhs)
    zc_new = _stack_heads([_dot_nt(hs(qc, h), jnp.concatenate([hs(kc_new, h), zpad], axis=0)) for h in range(N_C)])
    zc_pages = [_stack_heads([_dot(hs(qc, h), fk[p][h].astype(BF)) for h in range(N_C)]) for p in range(n)]

    mask_new = s_idx < t_idx
    ls_new, lhs_new = _sb_pre(zb_new, mask_new)
    ls_all, lhs_all = _sb_pre(zb, None)
    cs_new = _dot(lhs_new, sb_rhs)
    cs_all = _dot(lhs_all, sb_rhs)

    zc_new = jnp.where(s_idx <= t_idx, zc_new + g_q - g_k, NEG)
    r_carry = jnp.zeros((SUBLANE, PAGE), F32)
    for p in reversed(range(n)):
        csp = cs[p * SUBLANE:(p + 1) * SUBLANE]
        r_page = csp[:, :PAGE] + r_carry
        r_carry = r_carry + csp[:, PAGE:]
        bias = _stack_heads([jnp.broadcast_to(r_page[h:h + 1], (T, PAGE)) for h in range(N_C)]) + g_q
        zc_pages[p] = zc_pages[p] + bias
    zc = jnp.concatenate(zc_pages, axis=1)
    m = jnp.maximum(jnp.max(zc_new, axis=1, keepdims=True), jnp.max(zc, axis=1, keepdims=True))
    p_new = jnp.exp(zc_new - m)
    p_all = jnp.exp(zc - m)
    den = jnp.sum(p_new, axis=1, keepdims=True) + jnp.sum(p_all, axis=1, keepdims=True)
    p_new = p_new.astype(BF)
    p_all = p_all.astype(BF)
    accs = [_dot(p_new[h * T:(h + 1) * T], jnp.concatenate([hs(vc_new, h), zpad], axis=0)) for h in range(N_C)]
    for p in range(n):
        accs = [accs[h] + _dot_nt(page(p_all, p)[h * T:(h + 1) * T], fv[p][h].astype(BF)) for h in range(N_C)]

    a_new, c = _sb_post(ls_new, cs_new, None, mask_new)
    a_all, _ = _sb_post(ls_all, cs_all, c, None)
    acc = [_dot(a_new[h * T:(h + 1) * T], jnp.concatenate([hs(vb_new, h), zpad], axis=0)) for h in range(N_B)]
    for p in range(n):
        acc = [acc[h] + _dot_nt(page(a_all, p)[h * T:(h + 1) * T], sbv[p][h].astype(BF)) for h in range(N_B)]

    out = _stack_heads(accs) / den
    oc_ref[...] = jnp.concatenate([out[h * T:(h + 1) * T] for h in range(N_C)], axis=1)
    ob_ref[...] = jnp.concatenate(acc, axis=1)

    @pl.when(b == n_seq - 1)
    def _():
        for cp in page_copies(nxt, 1 - slot):
            cp.wait()


def _decode(page_table, qb, qc, kb, vb, kc, vc, lf, sbkT, sbvT, fkT, fvT, flfT, *, layer, t_new):
    nb, n_pages = page_table.shape
    N = qb.shape[0]
    tok = lambda w: pl.BlockSpec((t_new, w), lambda b, pt: (b, 0))
    hbm = pl.BlockSpec(memory_space=pl.ANY)
    kv_buf = pltpu.VMEM((2, n_pages, N_B, HEAD_DIM, PAGE), F32)
    return pl.pallas_call(
        functools.partial(_decode_kernel, layer=layer, n_pages=n_pages, t_new=t_new, n_seq=nb),
        grid_spec=pltpu.PrefetchScalarGridSpec(
            num_scalar_prefetch=1,
            grid=(nb,),
            in_specs=[tok(W_B), tok(W_C), tok(W_B), tok(W_B), tok(W_C), tok(W_C), tok(LANE)] + [hbm] * 5,
            out_specs=[tok(W_B), tok(W_C)],
            scratch_shapes=[kv_buf, kv_buf, kv_buf, kv_buf, pltpu.VMEM((2, n_pages, N_C, PAGE), F32),
                            pltpu.SemaphoreType.DMA((2,))],
        ),
        out_shape=[jax.ShapeDtypeStruct((N, W_B), F32), jax.ShapeDtypeStruct((N, W_C), F32)],
        compiler_params=_cparams("arbitrary"),
        name="decode_attention",
    )(page_table, qb, qc, kb, vb, kc, vc, lf, sbkT, sbvT, fkT, fvT, flfT)


TM_PROMPT = 512
TM_SAMPLE = 512


def kernel(x_prompt, x_sample, cache_sb_k, cache_sb_v, cache_fox_k, cache_fox_v, cache_fox_logf, page_table, g_attn, w_in, b_f, g_v, w_s, b_s, g_mix, w_o, g_ffn, w_ffn_in, w_ffn_out, g_final):
    B, S, D = x_prompt.shape
    DB, T, _ = x_sample.shape
    depth = g_attn.shape[0]
    assert D == D_MODEL and S % TM_PROMPT == 0 and (DB * T) % TM_SAMPLE == 0
    assert CHUNK % T == 0 and TM_SAMPLE % CHUNK == 0 and cache_sb_k.shape[2] == PAGE

    kv_t = lambda c: jnp.transpose(c, (0, 1, 3, 4, 2))
    sbkT, sbvT, fkT, fvT = kv_t(cache_sb_k), kv_t(cache_sb_v), kv_t(cache_fox_k), kv_t(cache_fox_v)
    flfT = jnp.transpose(cache_fox_logf, (0, 1, 3, 2))

    yp = x_prompt
    ys = x_sample.reshape(DB * T, D)
    stacked = None
    outs_s = [[] for _ in range(6)]

    w_tokT, w_kvT, bf_col, bf_row = _split_w_in(w_in, b_f)
    wo, wfi, wfo = w_o.astype(BF), w_ffn_in.astype(BF), w_ffn_out.astype(BF)
    vec = lambda a: a.reshape(depth, 1, a.shape[-1])
    ga, gv, gm, gf, gfin = vec(g_attn), vec(g_v), vec(g_mix), vec(g_ffn), g_final.reshape(1, D)
    wmix_p, bmix_p = _mix_operands_prompt(w_s, b_s)
    wmix_s, bmix_s = _mix_operands_sample(w_s, b_s, T)

    for l in range(depth):
        last = l == depth - 1

        u, va, qb, qc, lfc, *stacked = _inproj_prompt(
            yp, ga, w_tokT, w_kvT, bf_col, bf_row, gv, stacked, layer=l, tm=TM_PROMPT)
        kbT, vbT, kcT, vcT, lfr = stacked
        f_row, f_col = _fcum(lfr, lfc, layer=l)
        c_out = _fox_prompt(qc, kcT, vcT, f_row, f_col, layer=l)
        b_out = _sb_prompt(qb, kbT, vbT, layer=l)
        flat = lambda a: a.reshape(B * S, a.shape[-1])
        yp = _merge_ffn(flat(yp), flat(u), flat(va), flat(b_out), flat(c_out), wmix_p, bmix_p, gm, wo,
                        gf, wfi, wfo, gfin, layer=l, tm=TM_PROMPT, final_norm=last).reshape(B, S, D)

        u, va, qb, qc, lfc, kb, vb, kc, vc = _inproj_sample(
            ys, ga, w_tokT, w_kvT, bf_col, gv, layer=l, tm=TM_SAMPLE)
        b_out, c_out = _decode(page_table, qb, qc, kb, vb, kc, vc, lfc, sbkT, sbvT, fkT, fvT, flfT,
                               layer=l, t_new=T)
        ys = _merge_ffn(ys, u, va, b_out, c_out, wmix_s, bmix_s, gm, wo,
                        gf, wfi, wfo, gfin, layer=l, tm=TM_SAMPLE, final_norm=last)
        for dst, a in zip(outs_s, (kb, vb, kc, vc)):
            dst.append(a.reshape(DB, T, N_B, HEAD_DIM))
        outs_s[4].append(lfc[:, :N_C].reshape(DB, T, N_C))
        outs_s[5].append(va.reshape(DB, T, W_A))

    to_seq_major = lambda a: jnp.transpose(a.reshape(depth, B, -1, HEAD_DIM, S), (0, 1, 4, 2, 3))
    kbT, vbT, kcT, vcT, lfr = stacked
    p_lf = jnp.transpose(lfr[:, :, :N_C, :], (0, 1, 3, 2))
    return (yp, ys.reshape(DB, T, D),
            to_seq_major(kbT), to_seq_major(vbT), to_seq_major(kcT), to_seq_major(vcT), p_lf,
            jnp.stack(outs_s[0]), jnp.stack(outs_s[1]), jnp.stack(outs_s[2]), jnp.stack(outs_s[3]),
            jnp.stack(outs_s[4]), jnp.stack(outs_s[5]))
```

```python
import functools
import math

import jax
import jax.numpy as jnp
import numpy as np
from jax import lax
from jax.experimental import pallas as pl
from jax.experimental.pallas import tpu as pltpu

D_MODEL = 1024
HEAD_DIM = 64
N_A, N_B, N_C = 4, 6, 6
W_A, W_B, W_C = N_A * HEAD_DIM, N_B * HEAD_DIM, N_C * HEAD_DIM
CHUNK = 128
PAGE = 128
D_FF = 2816
EPS = 1e-6
NEG = -1e30
LOG2E = 1.4426950408889634
LANE = 128
SUBLANE = 8
BLK = 128
VMEM_LIMIT = 48 * 1024 * 1024
BF = jnp.bfloat16
F32 = jnp.float32

_O_U, _O_VA, _O_QB, _O_KB, _O_VB = 0, W_A, 2 * W_A, 2 * W_A + W_B, 2 * W_A + 2 * W_B
_O_QC = 2 * W_A + 3 * W_B
_O_KC, _O_VC, _O_F = _O_QC + W_C, _O_QC + 2 * W_C, _O_QC + 3 * W_C


def _cparams(*sem):
    return pltpu.CompilerParams(dimension_semantics=sem, vmem_limit_bytes=VMEM_LIMIT)


def _gelu(x):
    return 0.5 * x * (1.0 + jnp.tanh(math.sqrt(2.0 / math.pi) * (x + 0.044715 * (x * x * x))))


def _log_sigmoid(x):
    return jnp.minimum(x, 0.0) - jnp.log(1.0 + jnp.exp(-jnp.abs(x)))


def _rms(x, g):
    return x * lax.rsqrt(jnp.mean(x * x, axis=-1, keepdims=True) + EPS) * g


def _dot(a, b):
    return jnp.dot(a, b, preferred_element_type=F32)


def _dot_nt(a, b):
    return lax.dot_general(a, b, (((1,), (1,)), ((), ())), preferred_element_type=F32)


def _inproj_tok_epilogue(zt, bf_col, gv, u_ref, va_ref, qb_ref, qc_ref, lfc_ref, qc_scale):
    u_ref[...] = _gelu(zt[:, 0:W_A])
    va_ref[...] = _rms(_gelu(zt[:, W_A:2 * W_A]), gv)
    qb_ref[...] = (zt[:, 2 * W_A:2 * W_A + W_B] * (LOG2E * HEAD_DIM ** -0.5)).astype(qb_ref.dtype)
    qc_ref[...] = (zt[:, 2 * W_A + W_B:2 * W_A + W_B + W_C] * qc_scale).astype(qc_ref.dtype)
    o = 2 * W_A + W_B + W_C
    lfc_ref[...] = _log_sigmoid(zt[:, o:o + LANE] + bf_col)


def _inproj_prompt_kernel(x_ref, g_ref, wt_ref, wT_ref, bfc_ref, bfr_ref, gv_ref, *refs, layer, init_stack):
    if not init_stack:
        refs = refs[5:]
    u_ref, va_ref, qb_ref, qc_ref, lfc_ref, kbT_ref, vbT_ref, kcT_ref, vcT_ref, lfr_ref = refs

    def put(ref, val):
        if init_stack:
            for d in range(ref.shape[0]):
                ref[d] = val if d == layer else jnp.zeros_like(val)
        else:
            ref[...] = val

    xn = _rms(x_ref[...], g_ref[...]).astype(BF)
    zt = _dot_nt(xn, wt_ref[...])
    _inproj_tok_epilogue(zt, bfc_ref[...], gv_ref[...], u_ref, va_ref, qb_ref, qc_ref, lfc_ref,
                         LOG2E * HEAD_DIM ** -0.5)
    zT = _dot_nt(wT_ref[...], xn)
    put(kbT_ref, zT[0:W_B])
    put(vbT_ref, zT[W_B:2 * W_B])
    put(kcT_ref, zT[2 * W_B:2 * W_B + W_C])
    put(vcT_ref, zT[2 * W_B + W_C:2 * W_B + 2 * W_C])
    o = 2 * W_B + 2 * W_C
    put(lfr_ref, _log_sigmoid(zT[o:o + SUBLANE] + bfr_ref[...]))


def _inproj_sample_kernel(x_ref, g_ref, wt_ref, wkv_ref, bfc_ref, gv_ref,
                          u_ref, va_ref, qb_ref, qc_ref, lfc_ref,
                          kb_ref, vb_ref, kc_ref, vc_ref):
    xn = _rms(x_ref[...], g_ref[...]).astype(BF)
    zt = _dot_nt(xn, wt_ref[...])
    _inproj_tok_epilogue(zt, bfc_ref[...], gv_ref[...], u_ref, va_ref, qb_ref, qc_ref, lfc_ref, HEAD_DIM ** -0.5)
    zk = _dot_nt(xn, wkv_ref[0:2 * W_B + 2 * W_C, :])
    kb_ref[...] = zk[:, 0:W_B]
    vb_ref[...] = zk[:, W_B:2 * W_B]
    kc_ref[...] = zk[:, 2 * W_B:2 * W_B + W_C]
    vc_ref[...] = zk[:, 2 * W_B + W_C:2 * W_B + 2 * W_C]


def _split_w_in(w_in, b_f):
    depth = w_in.shape[0]
    wT = jnp.swapaxes(w_in, 1, 2)
    wfT = wT[:, _O_F:]
    pad_rows = lambda a, n: jnp.pad(a, ((0, 0), (0, n - a.shape[1]), (0, 0)))
    w_tokT = jnp.concatenate([wT[:, _O_U:_O_QB], wT[:, _O_QB:_O_KB], wT[:, _O_QC:_O_KC],
                              pad_rows(wfT, LANE)], axis=1).astype(BF)
    w_kvT = jnp.concatenate([wT[:, _O_KB:_O_QC], wT[:, _O_KC:_O_F], pad_rows(wfT, SUBLANE)], axis=1).astype(BF)
    bf_col = jnp.pad(b_f, ((0, 0), (0, LANE - N_C))).reshape(depth, 1, LANE)
    bf_row = jnp.pad(b_f, ((0, 0), (0, SUBLANE - N_C))).reshape(depth, SUBLANE, 1)
    return w_tokT, w_kvT, bf_col, bf_row


def _const_spec(shape):
    nd = len(shape)
    return pl.BlockSpec(shape, lambda *_: (0,) * nd, pipeline_mode=pl.Buffered(1))


def _layer_spec(arr, layer):
    nd = arr.ndim - 1
    return pl.BlockSpec((None,) + arr.shape[1:], lambda *_: (layer,) + (0,) * nd, pipeline_mode=pl.Buffered(1))


def _inproj_prompt(x, g, w_tokT, w_kvT, bf_col, bf_row, gv, stacked, *, layer, tm):
    B, S, D = x.shape
    depth = g.shape[0]
    init_stack = stacked is None
    row = lambda w: pl.BlockSpec((None, tm, w), lambda b, s: (b, s, 0))
    if init_stack:
        colT = lambda h: pl.BlockSpec((depth, None, h, tm), lambda b, s: (0, b, 0, s))
    else:
        colT = lambda h: pl.BlockSpec((None, None, h, tm), lambda b, s: (layer, b, 0, s))
    f = lambda *sh: jax.ShapeDtypeStruct(sh, F32)
    n_in = 7
    return pl.pallas_call(
        functools.partial(_inproj_prompt_kernel, layer=layer, init_stack=init_stack),
        grid=(B, S // tm),
        in_specs=[row(D)] + [_layer_spec(a, layer) for a in (g, w_tokT, w_kvT, bf_col, bf_row, gv)]
                 + ([] if init_stack else [pl.BlockSpec(memory_space=pl.ANY)] * 5),
        out_specs=[row(W_A), row(W_A), row(W_B), row(W_C), row(LANE),
                   colT(W_B), colT(W_B), colT(W_C), colT(W_C), colT(SUBLANE)],
        out_shape=[f(B, S, W_A), f(B, S, W_A), jax.ShapeDtypeStruct((B, S, W_B), BF),
                   jax.ShapeDtypeStruct((B, S, W_C), BF), f(B, S, LANE),
                   f(depth, B, W_B, S), f(depth, B, W_B, S), f(depth, B, W_C, S), f(depth, B, W_C, S),
                   f(depth, B, SUBLANE, S)],
        input_output_aliases={} if init_stack else {n_in + j: 5 + j for j in range(5)},
        compiler_params=_cparams("parallel", "parallel"),
        name="inproj_prompt",
    )(x, g, w_tokT, w_kvT, bf_col, bf_row, gv, *([] if init_stack else stacked))


def _inproj_sample(x, g, w_tokT, w_kvT, bf_col, gv, *, layer, tm):
    N, D = x.shape
    row = lambda w: pl.BlockSpec((tm, w), lambda i: (i, 0))
    f = lambda *sh: jax.ShapeDtypeStruct(sh, F32)
    return pl.pallas_call(
        _inproj_sample_kernel,
        grid=(N // tm,),
        in_specs=[row(D)] + [_layer_spec(a, layer) for a in (g, w_tokT, w_kvT, bf_col, gv)],
        out_specs=[row(W_A), row(W_A), row(W_B), row(W_C), row(LANE),
                   row(W_B), row(W_B), row(W_C), row(W_C)],
        out_shape=[f(N, W_A), f(N, W_A), f(N, W_B), f(N, W_C), f(N, LANE),
                   f(N, W_B), f(N, W_B), f(N, W_C), f(N, W_C)],
        compiler_params=_cparams("parallel"),
        name="inproj_sample",
    )(x, g, w_tokT, w_kvT, bf_col, gv)


FF_CHUNK = 256


def _merge_ffn_kernel(x_ref, u_ref, va_ref, b_ref, c_ref, wmix_ref, bmix_ref, gmix_ref, wo_ref,
                      gffn_ref, win_ref, wout_ref, gfin_ref, y_ref, *, tm, final_norm):
    lane_grp = lax.broadcasted_iota(jnp.int32, (CHUNK, W_A), 1) // HEAD_DIM
    a_rows = []
    for r in range(tm // CHUNK):
        va = va_ref[r * CHUNK:(r + 1) * CHUNK, :]
        mixed = bmix_ref[...]
        for g in range(N_A):
            vg = jnp.where(lane_grp == g, va, 0.0).astype(BF)
            mixed = mixed + _dot(wmix_ref[g], vg)
        a_rows.append(u_ref[r * CHUNK:(r + 1) * CHUNK, :] * mixed)
    a_out = jnp.concatenate(a_rows, axis=0) if len(a_rows) > 1 else a_rows[0]
    gm = gmix_ref[...]
    cat = jnp.concatenate([_rms(a_out, gm[:, 0:W_A]).astype(BF),
                           _rms(b_ref[...], gm[:, W_A:W_A + W_B]).astype(BF),
                           _rms(c_ref[...], gm[:, W_A + W_B:]).astype(BF)], axis=1)
    y = x_ref[...] + _dot(cat, wo_ref[...])
    yn = _rms(y, gffn_ref[...]).astype(BF)
    acc = y
    for c in range(D_FF // FF_CHUNK):
        h = _dot(yn, win_ref[:, c * FF_CHUNK:(c + 1) * FF_CHUNK])
        gate = _dot(yn, win_ref[:, D_FF + c * FF_CHUNK:D_FF + (c + 1) * FF_CHUNK])
        act = (h * jax.nn.sigmoid(h) * gate).astype(BF)
        acc = acc + _dot(act, wout_ref[c * FF_CHUNK:(c + 1) * FF_CHUNK, :])
    if final_norm:
        acc = _rms(acc, gfin_ref[...])
    y_ref[...] = acc


def _mix_operands_prompt(w_s, b_s):
    wmix = jnp.tril(w_s).astype(BF)
    bmix = jnp.repeat(jnp.swapaxes(b_s, 1, 2), HEAD_DIM, axis=2)
    return wmix, bmix


def _mix_operands_sample(w_s, b_s, t):
    depth = w_s.shape[0]
    reps = CHUNK // t
    small = jnp.tril(w_s[:, :, :t, :t])
    eye = jnp.eye(reps, dtype=small.dtype)
    wmix = jnp.einsum("rs,dgij->dgrisj", eye, small).reshape(depth, N_A, CHUNK, CHUNK).astype(BF)
    bmix = jnp.tile(jnp.repeat(jnp.swapaxes(b_s[:, :, :t], 1, 2), HEAD_DIM, axis=2), (1, reps, 1))
    return wmix, bmix


def _merge_ffn(x, u, va, b_out, c_out, wmix, bmix, gmix, wo, gffn, win, wout, gfin, *, layer, tm, final_norm):
    N, D = x.shape
    row = lambda w: pl.BlockSpec((tm, w), lambda i: (i, 0))
    params = (wmix, bmix, gmix, wo, gffn, win, wout)
    return pl.pallas_call(
        functools.partial(_merge_ffn_kernel, tm=tm, final_norm=final_norm),
        grid=(N // tm,),
        in_specs=[row(D), row(W_A), row(W_A), row(W_B), row(W_C)]
                 + [_layer_spec(a, layer) for a in params] + [_const_spec((1, D))],
        out_specs=row(D),
        out_shape=jax.ShapeDtypeStruct((N, D), F32),
        compiler_params=_cparams("parallel"),
        name="merge_ffn",
    )(x, u, va, b_out, c_out, *params, gfin)


def _split3(x):
    hi = x.astype(BF)
    r = x - hi.astype(F32)
    mid = r.astype(BF)
    lo = (r - mid.astype(F32)).astype(BF)
    return hi, mid, lo


def _fcum_kernel(lfr_ref, lfc_ref, fr_ref, fc_ref, *, S):
    ii = lax.broadcasted_iota(jnp.int32, (BLK, BLK), 0)
    jj = lax.broadcasted_iota(jnp.int32, (BLK, BLK), 1)
    t_incl = jnp.where(ii <= jj, 1.0, 0.0).astype(BF)
    l_incl = jnp.where(jj <= ii, 1.0, 0.0).astype(BF)
    rhs1 = jnp.concatenate([t_incl, jnp.ones((BLK, BLK), BF)], axis=1)
    rhs = jnp.concatenate([rhs1, rhs1, rhs1], axis=0)
    carry_r = jnp.zeros((SUBLANE, BLK), F32)
    carry_c = jnp.zeros((1, LANE), F32)
    for blk in range(S // BLK):
        sl = slice(blk * BLK, (blk + 1) * BLK)
        cs = _dot(jnp.concatenate(_split3(lfr_ref[:, sl]), axis=1), rhs)
        fr_ref[:, sl] = (cs[:, :BLK] + carry_r) * LOG2E
        carry_r = carry_r + cs[:, BLK:]
        hi, mid, lo = _split3(lfc_ref[sl, :])
        fc = _dot(l_incl, hi) + _dot(l_incl, mid) + _dot(l_incl, lo) + carry_c
        fc_ref[sl, :] = fc * LOG2E
        carry_c = fc[BLK - 1:BLK, :]


def _fcum(lf_row, lf_col, *, layer):
    _, B, _, S = lf_row.shape
    return pl.pallas_call(
        functools.partial(_fcum_kernel, S=S),
        grid=(B,),
        in_specs=[pl.BlockSpec((None, None, SUBLANE, S), lambda b: (layer, b, 0, 0)),
                  pl.BlockSpec((None, S, LANE), lambda b: (b, 0, 0))],
        out_specs=[pl.BlockSpec((None, SUBLANE, S), lambda b: (b, 0, 0)),
                   pl.BlockSpec((None, S, LANE), lambda b: (b, 0, 0))],
        out_shape=[jax.ShapeDtypeStruct((B, SUBLANE, S), F32), jax.ShapeDtypeStruct((B, S, LANE), F32)],
        compiler_params=_cparams("parallel"),
        name="fox_cumsum",
    )(lf_row, lf_col)


K_TILE_BLOCKS = 4
FOX_K_TILE_BLOCKS = 8


def _ktile(ref, h, start, width):
    return ref[h * HEAD_DIM:(h + 1) * HEAD_DIM, pl.ds(pl.multiple_of(start, BLK), width)]


def _fox_prompt_kernel(q_ref, kT_ref, vT_ref, fr_ref, fc_ref, o_ref, m_scr, acc_scr, fq_scr, *, kw):
    qi = pl.program_id(1)
    nblk = kw // BLK
    gd = qi // nblk
    causal = lax.broadcasted_iota(jnp.int32, (BLK, BLK), 1) <= lax.broadcasted_iota(jnp.int32, (BLK, BLK), 0)
    qs = [q_ref[:, h * HEAD_DIM:(h + 1) * HEAD_DIM].astype(BF) for h in range(N_C)]
    for h in range(N_C):
        fq_scr[h] = jnp.broadcast_to(fc_ref[:, h:h + 1], (BLK, LANE))

    def step(g, width, mask_last, first):
        ones_rows = jnp.ones((SUBLANE, width), BF)
        nb = width // BLK
        st = [dict() for _ in range(N_C)]
        for h, s in _issue_order(N_C, 2, FOX_SKEW):
            d = st[h]
            if s == 0:
                frow = fr_ref[h:h + 1, pl.ds(pl.multiple_of(g * kw, BLK), width)]
                qk = _dot(qs[h], _ktile(kT_ref, h, g * kw, width).astype(BF))
                fq = fq_scr[h]
                z = [qk[:, b * BLK:(b + 1) * BLK] + fq - frow[:, b * BLK:(b + 1) * BLK] for b in range(nb)]
                if mask_last is not None:
                    z[-1] = jnp.where(mask_last, z[-1], NEG)
                zmax = z[0]
                for zb in z[1:]:
                    zmax = jnp.maximum(zmax, zb)
                m_new = jnp.broadcast_to(jnp.max(zmax, axis=1, keepdims=True), (BLK, LANE))
                if not first:
                    m_old = m_scr[h]
                    m_new = jnp.maximum(m_old, m_new)
                    d["alpha"] = jnp.exp2(m_old - m_new)[:, :HEAD_DIM + SUBLANE]
                p = [jnp.exp2(zb - m_new).astype(BF) for zb in z]
                d["p"] = jnp.concatenate(p, axis=1) if nb > 1 else p[0]
                m_scr[h] = m_new
            else:
                v_ext = jnp.concatenate([_ktile(vT_ref, h, g * kw, width).astype(BF), ones_rows], axis=0)
                pv = _dot_nt(d["p"], v_ext)
                acc_scr[h] = pv if first else acc_scr[h] * d["alpha"] + pv
        return 0

    diag = [functools.partial(step, gd, (w + 1) * BLK, causal, True) for w in range(nblk)]
    if nblk > 1:
        lax.switch(qi % nblk, diag)
    else:
        diag[0]()
    lax.fori_loop(0, gd, lambda g, _: step(g, kw, None, False), 0)
    outs = []
    for h in range(N_C):
        acc = acc_scr[h]
        outs.append(acc[:, :HEAD_DIM] / acc[:, HEAD_DIM:HEAD_DIM + 1])
    o_ref[...] = jnp.concatenate(outs, axis=1)


def _fox_prompt(q, kT, vT, f_row, f_col, *, layer):
    B, S, W = q.shape
    kw = min(FOX_K_TILE_BLOCKS * BLK, S)
    return pl.pallas_call(
        functools.partial(_fox_prompt_kernel, kw=kw),
        grid=(B, S // BLK),
        in_specs=[pl.BlockSpec((None, BLK, W), lambda b, i: (b, i, 0)),
                  pl.BlockSpec((None, None, W, S), lambda b, i: (layer, b, 0, 0)),
                  pl.BlockSpec((None, None, W, S), lambda b, i: (layer, b, 0, 0)),
                  pl.BlockSpec((None, SUBLANE, S), lambda b, i: (b, 0, 0)),
                  pl.BlockSpec((None, BLK, LANE), lambda b, i: (b, i, 0))],
        out_specs=pl.BlockSpec((None, BLK, W), lambda b, i: (b, i, 0)),
        out_shape=jax.ShapeDtypeStruct((B, S, W), F32),
        scratch_shapes=[pltpu.VMEM((N_C, BLK, LANE), F32), pltpu.VMEM((N_C, BLK, HEAD_DIM + SUBLANE), F32),
                        pltpu.VMEM((N_C, BLK, LANE), F32)],
        compiler_params=_cparams("parallel", "arbitrary"),
        name="fox_prompt",
    )(q, kT, vT, f_row, f_col)


def _suffix_sum_rhs():
    k = lax.broadcasted_iota(jnp.int32, (2 * BLK, 2 * BLK), 0) % BLK
    n = lax.broadcasted_iota(jnp.int32, (2 * BLK, 2 * BLK), 1)
    return jnp.where((n >= BLK) | (k > n), 1.0, 0.0).astype(BF)


def _sb_pre(z, mask_last):
    nb = z.shape[1] // BLK
    sp = jnp.log(1.0 + jnp.exp2(-jnp.abs(z))) * LOG2E
    ls = jnp.minimum(z, 0.0) - sp
    l1m = ls - z
    blk = lambda x, b: x[:, b * BLK:(b + 1) * BLK]
    lhs = []
    for b in range(nb):
        lb = blk(l1m, b)
        if mask_last is not None and b == nb - 1:
            lb = jnp.where(mask_last, lb, 0.0)
        hi = lb.astype(BF)
        lo = (lb - hi.astype(F32)).astype(BF)
        lhs.append(jnp.concatenate([hi, lo], axis=1))
    return ls, (jnp.concatenate(lhs, axis=0) if nb > 1 else lhs[0])


def _sb_post(ls, cs, c, mask_last):
    r, n = ls.shape
    nb = n // BLK
    a = [None] * nb
    for b in reversed(range(nb)):
        csb = cs[b * r:(b + 1) * r]
        e = ls[:, b * BLK:(b + 1) * BLK] + csb[:, :BLK]
        ab = jnp.exp2(e if c is None else e + c)
        if mask_last is not None and b == nb - 1:
            ab = jnp.where(mask_last, ab, 0.0)
        a[b] = ab.astype(BF)
        c = csb[:, BLK:] if c is None else c + csb[:, BLK:]
    return (jnp.concatenate(a, axis=1) if nb > 1 else a[0]), c


def _issue_order(n_chains, n_stages, dist):
    items = [(h + s * dist, s, h) for h in range(n_chains) for s in range(n_stages)]
    return [(h, s) for _, s, h in sorted(items)]


SB_SKEW = 4
FOX_SKEW = 6


def _sb_prompt_kernel(q_ref, kT_ref, vT_ref, rhs_ref, o_ref, c_scr, acc_scr, *, kw):
    qi = pl.program_id(1)
    nblk = kw // BLK
    gd = qi // nblk
    strict = lax.broadcasted_iota(jnp.int32, (BLK, BLK), 1) < lax.broadcasted_iota(jnp.int32, (BLK, BLK), 0)
    qs = [q_ref[:, h * HEAD_DIM:(h + 1) * HEAD_DIM].astype(BF) for h in range(N_B)]

    def step(g, width, mask_last, first):
        st = [dict() for _ in range(N_B)]
        for h, s in _issue_order(N_B, 3, SB_SKEW):
            d = st[h]
            if s == 0:
                z = _dot(qs[h], _ktile(kT_ref, h, g * kw, width).astype(BF))
                d["ls"], d["lhs"] = _sb_pre(z, mask_last)
            elif s == 1:
                cs = _dot(d["lhs"], rhs_ref[...])
                d["a"], c = _sb_post(d["ls"], cs, None if first else c_scr[h], mask_last)
                c_scr[h] = c
            else:
                pv = _dot_nt(d["a"], _ktile(vT_ref, h, g * kw, width).astype(BF))
                acc_scr[h] = pv if first else acc_scr[h] + pv
        return 0

    diag = [functools.partial(step, gd, (w + 1) * BLK, strict, True) for w in range(nblk)]
    if nblk > 1:
        lax.switch(qi % nblk, diag)
    else:
        diag[0]()
    lax.fori_loop(0, gd, lambda i, _: step(gd - 1 - i, kw, None, False), 0)
    o_ref[...] = jnp.concatenate([acc_scr[h] for h in range(N_B)], axis=1)


def _sb_prompt(q, kT, vT, *, layer):
    B, S, W = q.shape
    kw = min(K_TILE_BLOCKS * BLK, S)
    return pl.pallas_call(
        functools.partial(_sb_prompt_kernel, kw=kw),
        grid=(B, S // BLK),
        in_specs=[pl.BlockSpec((None, BLK, W), lambda b, i: (b, i, 0)),
                  pl.BlockSpec((None, None, W, S), lambda b, i: (layer, b, 0, 0)),
                  pl.BlockSpec((None, None, W, S), lambda b, i: (layer, b, 0, 0)),
                  _const_spec((2 * BLK, 2 * BLK))],
        out_specs=pl.BlockSpec((None, BLK, W), lambda b, i: (b, i, 0)),
        out_shape=jax.ShapeDtypeStruct((B, S, W), F32),
        scratch_shapes=[pltpu.VMEM((N_B, BLK, BLK), F32), pltpu.VMEM((N_B, BLK, HEAD_DIM), F32)],
        compiler_params=_cparams("parallel", "arbitrary"),
        name="sb_prompt",
    )(q, kT, vT, _suffix_sum_rhs())


def _stack_heads(parts):
    return jnp.concatenate(parts, axis=0)


def _decode_kernel(pt_ref, qb_ref, qc_ref, kb_ref, vb_ref, kc_ref, vc_ref, lf_ref, *refs,
                   layer, n_pages, t_new, n_seq, seq=None):
    n = n_pages
    (sbk_hbm, sbv_hbm, fk_hbm, fv_hbm, flf_hbm, ob_ref, oc_ref,
     sbk_buf, sbv_buf, fk_buf, fv_buf, flf_buf, sem) = refs
    b = pl.program_id(0) if seq is None else seq
    slot = lax.rem(b, 2)

    def page_copies(seq, dst_slot):
        cps = []
        for p in range(n):
            pg = pt_ref[seq, p]
            for src, dst in ((sbk_hbm, sbk_buf), (sbv_hbm, sbv_buf), (fk_hbm, fk_buf), (fv_hbm, fv_buf),
                             (flf_hbm, flf_buf)):
                cps.append(pltpu.make_async_copy(src.at[layer, pg], dst.at[dst_slot, p], sem.at[dst_slot]))
        return cps

    @pl.when(b == 0)
    def _():
        for cp in page_copies(0, 0):
            cp.start()

    nxt = jnp.minimum(b + 1, n_seq - 1)
    for cp in page_copies(nxt, 1 - slot):
        cp.start()
    for cp in page_copies(b, slot):
        cp.wait()

    sbk = [sbk_buf.at[slot, p] for p in range(n)]
    sbv = [sbv_buf.at[slot, p] for p in range(n)]
    fk = [fk_buf.at[slot, p] for p in range(n)]
    fv = [fv_buf.at[slot, p] for p in range(n)]
    flf = [flf_buf.at[slot, p] for p in range(n)]
    T = t_new
    rows = N_B * T
    hs = lambda x, h: x[:, h * HEAD_DIM:(h + 1) * HEAD_DIM]

    sb_rhs = _suffix_sum_rhs()
    kk = lax.broadcasted_iota(jnp.int32, (3 * PAGE, 2 * PAGE), 0) % PAGE
    nn = lax.broadcasted_iota(jnp.int32, (3 * PAGE, 2 * PAGE), 1)
    lf_rhs = jnp.where((nn >= PAGE) | (kk > nn), 1.0, 0.0).astype(BF)
    t_idx = lax.broadcasted_iota(jnp.int32, (rows, PAGE), 0) % T
    s_idx = lax.broadcasted_iota(jnp.int32, (rows, PAGE), 1)
    zpad = jnp.zeros((PAGE - T, HEAD_DIM), BF)
    ones_rows = jnp.ones((SUBLANE, PAGE), BF)

    page = lambda x, p: x[:, p * PAGE:(p + 1) * PAGE]

    qb = qb_ref[...].astype(BF)
    kb_new = kb_ref[...].astype(BF)
    vb_new = vb_ref[...].astype(BF)

    lf_new = lf_ref[...]
    g_rows = [lf_new[0:1]]
    for t in range(1, T):
        g_rows.append(g_rows[-1] + lf_new[t:t + 1])
    g_col = jnp.concatenate(g_rows, axis=0)
    eye = lax.broadcasted_iota(jnp.int32, (T, PAGE), 0) == lax.broadcasted_iota(jnp.int32, (T, PAGE), 1)
    g_q = _stack_heads([jnp.broadcast_to(g_col[:, h:h + 1], (T, PAGE)) for h in range(N_C)])
    g_k = _stack_heads([jnp.broadcast_to(
        jnp.sum(jnp.where(eye, jnp.broadcast_to(g_col[:, h:h + 1], (T, PAGE)), 0.0), axis=0, keepdims=True),
        (T, PAGE)) for h in range(N_C)])
    qc = qc_ref[...].astype(BF)
    kc_new = kc_ref[...].astype(BF)
    vc_new = vc_ref[...].astype(BF)

    zb_new = _stack_heads([_dot_nt(hs(qb, h), jnp.concatenate([hs(kb_new, h), zpad], axis=0)) for h in range(N_B)])
    zb = jnp.concatenate([_stack_heads([_dot(hs(qb, h), sbk[p][h].astype(BF)) for h in range(N_B)])
                          for p in range(n)], axis=1)

    lf_pad = jnp.zeros((SUBLANE - N_C, PAGE), F32)
    lf_all = jnp.concatenate([jnp.concatenate([flf[p][...], lf_pad], axis=0) for p in range(n)], axis=0)
    cs = _dot(jnp.concatenate(_split3(lf_all), axis=1), lf_rhs)
    zc_new = _stack_heads([_dot_nt(hs(qc, h), jnp.concatenate([hs(kc_new, h), zpad], axis=0)) for h in range(N_C)])
    zc_pages = [_stack_heads([_dot(hs(qc, h), fk[p][h].astype(BF)) for h in range(N_C)]) for p in range(n)]

    mask_new = s_idx < t_idx
    ls_new, lhs_new = _sb_pre(zb_new, mask_new)
    ls_all, lhs_all = _sb_pre(zb, None)
    cs_new = _dot(lhs_new, sb_rhs)
    cs_all = _dot(lhs_all, sb_rhs)

    zc_new = jnp.where(s_idx <= t_idx, zc_new + g_q - g_k, NEG)
    r_carry = jnp.zeros((SUBLANE, PAGE), F32)
    for p in reversed(range(n)):
        csp = cs[p * SUBLANE:(p + 1) * SUBLANE]
        r_page = csp[:, :PAGE] + r_carry
        r_carry = r_carry + csp[:, PAGE:]
        bias = _stack_heads([jnp.broadcast_to(r_page[h:h + 1], (T, PAGE)) for h in range(N_C)]) + g_q
        zc_pages[p] = zc_pages[p] + bias
    zc = jnp.concatenate(zc_pages, axis=1)
    m = jnp.maximum(jnp.max(zc_new, axis=1, keepdims=True), jnp.max(zc, axis=1, keepdims=True))
    p_new = jnp.exp(zc_new - m)
    p_all = jnp.exp(zc - m)
    den = jnp.sum(p_new, axis=1, keepdims=True) + jnp.sum(p_all, axis=1, keepdims=True)
    p_new = p_new.astype(BF)
    p_all = p_all.astype(BF)
    accs = [_dot(p_new[h * T:(h + 1) * T], jnp.concatenate([hs(vc_new, h), zpad], axis=0)) for h in range(N_C)]
    for p in range(n):
        accs = [accs[h] + _dot_nt(page(p_all, p)[h * T:(h + 1) * T], fv[p][h].astype(BF)) for h in range(N_C)]

    a_new, c = _sb_post(ls_new, cs_new, None, mask_new)
    a_all, _ = _sb_post(ls_all, cs_all, c, None)
    acc = [_dot(a_new[h * T:(h + 1) * T], jnp.concatenate([hs(vb_new, h), zpad], axis=0)) for h in range(N_B)]
    for p in range(n):
        acc = [acc[h] + _dot_nt(page(a_all, p)[h * T:(h + 1) * T], sbv[p][h].astype(BF)) for h in range(N_B)]

    out = _stack_heads(accs) / den
    oc_ref[...] = jnp.concatenate([out[h * T:(h + 1) * T] for h in range(N_C)], axis=1)
    ob_ref[...] = jnp.concatenate(acc, axis=1)

    @pl.when(b == n_seq - 1)
    def _():
        for cp in page_copies(nxt, 1 - slot):
            cp.wait()


def _decode(page_table, qb, qc, kb, vb, kc, vc, lf, sbkT, sbvT, fkT, fvT, flfT, *, layer, t_new):
    nb, n_pages = page_table.shape
    N = qb.shape[0]
    tok = lambda w: pl.BlockSpec((t_new, w), lambda b, pt: (b, 0))
    hbm = pl.BlockSpec(memory_space=pl.ANY)
    kv_buf = pltpu.VMEM((2, n_pages, N_B, HEAD_DIM, PAGE), F32)
    return pl.pallas_call(
        functools.partial(_decode_kernel, layer=layer, n_pages=n_pages, t_new=t_new, n_seq=nb),
        grid_spec=pltpu.PrefetchScalarGridSpec(
            num_scalar_prefetch=1,
            grid=(nb,),
            in_specs=[tok(W_B), tok(W_C), tok(W_B), tok(W_B), tok(W_C), tok(W_C), tok(LANE)] + [hbm] * 5,
            out_specs=[tok(W_B), tok(W_C)],
            scratch_shapes=[kv_buf, kv_buf, kv_buf, kv_buf, pltpu.VMEM((2, n_pages, N_C, PAGE), F32),
                            pltpu.SemaphoreType.DMA((2,))],
        ),
        out_shape=[jax.ShapeDtypeStruct((N, W_B), F32), jax.ShapeDtypeStruct((N, W_C), F32)],
        compiler_params=_cparams("arbitrary"),
        name="decode_attention",
    )(page_table, qb, qc, kb, vb, kc, vc, lf, sbkT, sbvT, fkT, fvT, flfT)


def _sb_decode_kernel(pt_ref, q_ref, kT_ref, vT_ref, rhs_ref,
                      qb_ref, qc_ref, kb_ref, vb_ref, kc_ref, vc_ref, lf_ref,
                      sbk_hbm, sbv_hbm, fk_hbm, fv_hbm, flf_hbm,
                      o_ref, ob_ref, oc_ref,
                      c_scr, acc_scr, sbk_buf, sbv_buf, fk_buf, fv_buf, flf_buf, sem,
                      *, kw, layer, n_pages, t_new, n_seq, nq):
    seq = pl.program_id(0) * nq + pl.program_id(1)
    _decode_kernel(pt_ref, qb_ref, qc_ref, kb_ref, vb_ref, kc_ref, vc_ref, lf_ref,
                   sbk_hbm, sbv_hbm, fk_hbm, fv_hbm, flf_hbm, ob_ref, oc_ref,
                   sbk_buf, sbv_buf, fk_buf, fv_buf, flf_buf, sem,
                   layer=layer, n_pages=n_pages, t_new=t_new, n_seq=n_seq, seq=seq)
    _sb_prompt_kernel(q_ref, kT_ref, vT_ref, rhs_ref, o_ref, c_scr, acc_scr, kw=kw)


def _sb_prompt_decode(q, kT, vT, page_table, qb, qc, kb, vb, kc, vc, lf, sbkT, sbvT, fkT, fvT, flfT,
                      *, layer, t_new):
    B, S, W = q.shape
    nq = S // BLK
    n_seq, n_pages = page_table.shape
    assert B * nq == n_seq, "one decode sequence per prompt query block"
    N = qb.shape[0]
    kw = min(K_TILE_BLOCKS * BLK, S)
    tok = lambda w: pl.BlockSpec((t_new, w), lambda b, i, pt: (b * nq + i, 0))
    hbm = pl.BlockSpec(memory_space=pl.ANY)
    kv_buf = pltpu.VMEM((2, n_pages, N_B, HEAD_DIM, PAGE), F32)
    return pl.pallas_call(
        functools.partial(_sb_decode_kernel, kw=kw, layer=layer, n_pages=n_pages, t_new=t_new, n_seq=n_seq, nq=nq),
        grid_spec=pltpu.PrefetchScalarGridSpec(
            num_scalar_prefetch=1,
            grid=(B, nq),
            in_specs=[pl.BlockSpec((None, BLK, W), lambda b, i, pt: (b, i, 0)),
                      pl.BlockSpec((None, None, W, S), lambda b, i, pt: (layer, b, 0, 0)),
                      pl.BlockSpec((None, None, W, S), lambda b, i, pt: (layer, b, 0, 0)),
                      _const_spec((2 * BLK, 2 * BLK)),
                      tok(W_B), tok(W_C), tok(W_B), tok(W_B), tok(W_C), tok(W_C), tok(LANE)] + [hbm] * 5,
            out_specs=[pl.BlockSpec((None, BLK, W), lambda b, i, pt: (b, i, 0)), tok(W_B), tok(W_C)],
            scratch_shapes=[pltpu.VMEM((N_B, BLK, BLK), F32), pltpu.VMEM((N_B, BLK, HEAD_DIM), F32),
                            kv_buf, kv_buf, kv_buf, kv_buf, pltpu.VMEM((2, n_pages, N_C, PAGE), F32),
                            pltpu.SemaphoreType.DMA((2,))],
        ),
        out_shape=[jax.ShapeDtypeStruct((B, S, W), F32),
                   jax.ShapeDtypeStruct((N, W_B), F32), jax.ShapeDtypeStruct((N, W_C), F32)],
        compiler_params=_cparams("arbitrary", "arbitrary"),
        name="sb_prompt_decode",
    )(page_table, q, kT, vT, _suffix_sum_rhs(), qb, qc, kb, vb, kc, vc, lf, sbkT, sbvT, fkT, fvT, flfT)


TM_PROMPT = 512
TM_SAMPLE = 512


def kernel(x_prompt, x_sample, cache_sb_k, cache_sb_v, cache_fox_k, cache_fox_v, cache_fox_logf, page_table, g_attn, w_in, b_f, g_v, w_s, b_s, g_mix, w_o, g_ffn, w_ffn_in, w_ffn_out, g_final):
    B, S, D = x_prompt.shape
    DB, T, _ = x_sample.shape
    depth = g_attn.shape[0]
    assert D == D_MODEL and S % TM_PROMPT == 0 and (DB * T) % TM_SAMPLE == 0
    assert CHUNK % T == 0 and TM_SAMPLE % CHUNK == 0 and cache_sb_k.shape[2] == PAGE

    kv_t = lambda c: jnp.transpose(c, (0, 1, 3, 4, 2))
    sbkT, sbvT, fkT, fvT = kv_t(cache_sb_k), kv_t(cache_sb_v), kv_t(cache_fox_k), kv_t(cache_fox_v)
    flfT = jnp.transpose(cache_fox_logf, (0, 1, 3, 2))

    yp = x_prompt
    ys = x_sample.reshape(DB * T, D)
    stacked = None
    outs_s = [[] for _ in range(6)]

    w_tokT, w_kvT, bf_col, bf_row = _split_w_in(w_in, b_f)
    wo, wfi, wfo = w_o.astype(BF), w_ffn_in.astype(BF), w_ffn_out.astype(BF)
    vec = lambda a: a.reshape(depth, 1, a.shape[-1])
    ga, gv, gm, gf, gfin = vec(g_attn), vec(g_v), vec(g_mix), vec(g_ffn), g_final.reshape(1, D)
    wmix_p, bmix_p = _mix_operands_prompt(w_s, b_s)
    wmix_s, bmix_s = _mix_operands_sample(w_s, b_s, T)

    for l in range(depth):
        last = l == depth - 1

        u, va, qb, qc, lfc, *stacked = _inproj_prompt(
            yp, ga, w_tokT, w_kvT, bf_col, bf_row, gv, stacked, layer=l, tm=TM_PROMPT)
        kbT, vbT, kcT, vcT, lfr = stacked
        f_row, f_col = _fcum(lfr, lfc, layer=l)
        c_out = _fox_prompt(qc, kcT, vcT, f_row, f_col, layer=l)

        us, vas, qbs, qcs, lfcs, kb, vb, kc, vc = _inproj_sample(
            ys, ga, w_tokT, w_kvT, bf_col, gv, layer=l, tm=TM_SAMPLE)
        b_out, bs_out, cs_out = _sb_prompt_decode(qb, kbT, vbT, page_table, qbs, qcs, kb, vb, kc, vc, lfcs,
                                                  sbkT, sbvT, fkT, fvT, flfT, layer=l, t_new=T)

        flat = lambda a: a.reshape(B * S, a.shape[-1])
        yp = _merge_ffn(flat(yp), flat(u), flat(va), flat(b_out), flat(c_out), wmix_p, bmix_p, gm, wo,
                        gf, wfi, wfo, gfin, layer=l, tm=TM_PROMPT, final_norm=last).reshape(B, S, D)
        ys = _merge_ffn(ys, us, vas, bs_out, cs_out, wmix_s, bmix_s, gm, wo,
                        gf, wfi, wfo, gfin, layer=l, tm=TM_SAMPLE, final_norm=last)
        lfc, va = lfcs, vas
        for dst, a in zip(outs_s, (kb, vb, kc, vc)):
            dst.append(a.reshape(DB, T, N_B, HEAD_DIM))
        outs_s[4].append(lfc[:, :N_C].reshape(DB, T, N_C))
        outs_s[5].append(va.reshape(DB, T, W_A))

    to_seq_major = lambda a: jnp.transpose(a.reshape(depth, B, -1, HEAD_DIM, S), (0, 1, 4, 2, 3))
    kbT, vbT, kcT, vcT, lfr = stacked
    p_lf = jnp.transpose(lfr[:, :, :N_C, :], (0, 1, 3, 2))
    return (yp, ys.reshape(DB, T, D),
            to_seq_major(kbT), to_seq_major(vbT), to_seq_major(kcT), to_seq_major(vcT), p_lf,
            jnp.stack(outs_s[0]), jnp.stack(outs_s[1]), jnp.stack(outs_s[2]), jnp.stack(outs_s[3]),
            jnp.stack(outs_s[4]), jnp.stack(outs_s[5]))
```

```python
import functools
import math

import jax
import jax.numpy as jnp
import numpy as np
from jax import lax
from jax.experimental import pallas as pl
from jax.experimental.pallas import tpu as pltpu

D_MODEL = 1024
HEAD_DIM = 64
N_A, N_B, N_C = 4, 6, 6
W_A, W_B, W_C = N_A * HEAD_DIM, N_B * HEAD_DIM, N_C * HEAD_DIM
CHUNK = 128
PAGE = 128
D_FF = 2816
EPS = 1e-6
NEG = -1e30
LOG2E = 1.4426950408889634
LANE = 128
SUBLANE = 8
BLK = 128
VMEM_LIMIT = 48 * 1024 * 1024
BF = jnp.bfloat16
F32 = jnp.float32

_O_U, _O_VA, _O_QB, _O_KB, _O_VB = 0, W_A, 2 * W_A, 2 * W_A + W_B, 2 * W_A + 2 * W_B
_O_QC = 2 * W_A + 3 * W_B
_O_KC, _O_VC, _O_F = _O_QC + W_C, _O_QC + 2 * W_C, _O_QC + 3 * W_C


def _cparams(*sem):
    return pltpu.CompilerParams(dimension_semantics=sem, vmem_limit_bytes=VMEM_LIMIT)


def _gelu(x):
    return 0.5 * x * (1.0 + jnp.tanh(math.sqrt(2.0 / math.pi) * (x + 0.044715 * (x * x * x))))


def _log_sigmoid(x):
    return jnp.minimum(x, 0.0) - jnp.log(1.0 + jnp.exp(-jnp.abs(x)))


def _rms(x, g):
    return x * lax.rsqrt(jnp.mean(x * x, axis=-1, keepdims=True) + EPS) * g


def _dot(a, b):
    return jnp.dot(a, b, preferred_element_type=F32)


def _dot_nt(a, b):
    return lax.dot_general(a, b, (((1,), (1,)), ((), ())), preferred_element_type=F32)


def _inproj_tok_epilogue(zt, bf_col, gv, u_ref, va_ref, qb_ref, qc_ref, lfc_ref, qc_scale):
    u_ref[...] = _gelu(zt[:, 0:W_A])
    va_ref[...] = _rms(_gelu(zt[:, W_A:2 * W_A]), gv)
    qb_ref[...] = (zt[:, 2 * W_A:2 * W_A + W_B] * (LOG2E * HEAD_DIM ** -0.5)).astype(qb_ref.dtype)
    qc_ref[...] = (zt[:, 2 * W_A + W_B:2 * W_A + W_B + W_C] * qc_scale).astype(qc_ref.dtype)
    o = 2 * W_A + W_B + W_C
    lfc_ref[...] = _log_sigmoid(zt[:, o:o + LANE] + bf_col)


def _inproj_prompt_kernel(x_ref, g_ref, wt_ref, wT_ref, bfc_ref, bfr_ref, gv_ref, *refs, layer, init_stack):
    if not init_stack:
        refs = refs[5:]
    u_ref, va_ref, qb_ref, qc_ref, lfc_ref, kbT_ref, vbT_ref, kcT_ref, vcT_ref, lfr_ref = refs

    def put(ref, val):
        if init_stack:
            for d in range(ref.shape[0]):
                ref[d] = val if d == layer else jnp.zeros_like(val)
        else:
            ref[...] = val

    xn = _rms(x_ref[...], g_ref[...]).astype(BF)
    zt = _dot_nt(xn, wt_ref[...])
    _inproj_tok_epilogue(zt, bfc_ref[...], gv_ref[...], u_ref, va_ref, qb_ref, qc_ref, lfc_ref,
                         LOG2E * HEAD_DIM ** -0.5)
    zT = _dot_nt(wT_ref[...], xn)
    put(kbT_ref, zT[0:W_B])
    put(vbT_ref, zT[W_B:2 * W_B])
    put(kcT_ref, zT[2 * W_B:2 * W_B + W_C])
    put(vcT_ref, zT[2 * W_B + W_C:2 * W_B + 2 * W_C])
    o = 2 * W_B + 2 * W_C
    put(lfr_ref, _log_sigmoid(zT[o:o + SUBLANE] + bfr_ref[...]))


def _inproj_sample_kernel(x_ref, g_ref, wt_ref, wkv_ref, bfc_ref, gv_ref,
                          u_ref, va_ref, qb_ref, qc_ref, lfc_ref,
                          kb_ref, vb_ref, kc_ref, vc_ref):
    xn = _rms(x_ref[...], g_ref[...]).astype(BF)
    zt = _dot_nt(xn, wt_ref[...])
    _inproj_tok_epilogue(zt, bfc_ref[...], gv_ref[...], u_ref, va_ref, qb_ref, qc_ref, lfc_ref, HEAD_DIM ** -0.5)
    zk = _dot_nt(xn, wkv_ref[0:2 * W_B + 2 * W_C, :])
    kb_ref[...] = zk[:, 0:W_B]
    vb_ref[...] = zk[:, W_B:2 * W_B]
    kc_ref[...] = zk[:, 2 * W_B:2 * W_B + W_C]
    vc_ref[...] = zk[:, 2 * W_B + W_C:2 * W_B + 2 * W_C]


def _split_w_in(w_in, b_f):
    depth = w_in.shape[0]
    wT = jnp.swapaxes(w_in, 1, 2)
    wfT = wT[:, _O_F:]
    pad_rows = lambda a, n: jnp.pad(a, ((0, 0), (0, n - a.shape[1]), (0, 0)))
    w_tokT = jnp.concatenate([wT[:, _O_U:_O_QB], wT[:, _O_QB:_O_KB], wT[:, _O_QC:_O_KC],
                              pad_rows(wfT, LANE)], axis=1).astype(BF)
    w_kvT = jnp.concatenate([wT[:, _O_KB:_O_QC], wT[:, _O_KC:_O_F], pad_rows(wfT, SUBLANE)], axis=1).astype(BF)
    bf_col = jnp.pad(b_f, ((0, 0), (0, LANE - N_C))).reshape(depth, 1, LANE)
    bf_row = jnp.pad(b_f, ((0, 0), (0, SUBLANE - N_C))).reshape(depth, SUBLANE, 1)
    return w_tokT, w_kvT, bf_col, bf_row


def _const_spec(shape):
    nd = len(shape)
    return pl.BlockSpec(shape, lambda *_: (0,) * nd, pipeline_mode=pl.Buffered(1))


def _layer_spec(arr, layer):
    nd = arr.ndim - 1
    return pl.BlockSpec((None,) + arr.shape[1:], lambda *_: (layer,) + (0,) * nd, pipeline_mode=pl.Buffered(1))


def _inproj_prompt(x, g, w_tokT, w_kvT, bf_col, bf_row, gv, stacked, *, layer, tm):
    B, S, D = x.shape
    depth = g.shape[0]
    init_stack = stacked is None
    row = lambda w: pl.BlockSpec((None, tm, w), lambda b, s: (b, s, 0))
    if init_stack:
        colT = lambda h: pl.BlockSpec((depth, None, h, tm), lambda b, s: (0, b, 0, s))
    else:
        colT = lambda h: pl.BlockSpec((None, None, h, tm), lambda b, s: (layer, b, 0, s))
    f = lambda *sh: jax.ShapeDtypeStruct(sh, F32)
    n_in = 7
    return pl.pallas_call(
        functools.partial(_inproj_prompt_kernel, layer=layer, init_stack=init_stack),
        grid=(B, S // tm),
        in_specs=[row(D)] + [_layer_spec(a, layer) for a in (g, w_tokT, w_kvT, bf_col, bf_row, gv)]
                 + ([] if init_stack else [pl.BlockSpec(memory_space=pl.ANY)] * 5),
        out_specs=[row(W_A), row(W_A), row(W_B), row(W_C), row(LANE),
                   colT(W_B), colT(W_B), colT(W_C), colT(W_C), colT(SUBLANE)],
        out_shape=[f(B, S, W_A), f(B, S, W_A), jax.ShapeDtypeStruct((B, S, W_B), BF),
                   jax.ShapeDtypeStruct((B, S, W_C), BF), f(B, S, LANE),
                   f(depth, B, W_B, S), f(depth, B, W_B, S), f(depth, B, W_C, S), f(depth, B, W_C, S),
                   f(depth, B, SUBLANE, S)],
        input_output_aliases={} if init_stack else {n_in + j: 5 + j for j in range(5)},
        compiler_params=_cparams("parallel", "parallel"),
        name="inproj_prompt",
    )(x, g, w_tokT, w_kvT, bf_col, bf_row, gv, *([] if init_stack else stacked))


def _inproj_sample(x, g, w_tokT, w_kvT, bf_col, gv, *, layer, tm):
    N, D = x.shape
    row = lambda w: pl.BlockSpec((tm, w), lambda i: (i, 0))
    f = lambda *sh: jax.ShapeDtypeStruct(sh, F32)
    return pl.pallas_call(
        _inproj_sample_kernel,
        grid=(N // tm,),
        in_specs=[row(D)] + [_layer_spec(a, layer) for a in (g, w_tokT, w_kvT, bf_col, gv)],
        out_specs=[row(W_A), row(W_A), row(W_B), row(W_C), row(LANE),
                   row(W_B), row(W_B), row(W_C), row(W_C)],
        out_shape=[f(N, W_A), f(N, W_A), f(N, W_B), f(N, W_C), f(N, LANE),
                   f(N, W_B), f(N, W_B), f(N, W_C), f(N, W_C)],
        compiler_params=_cparams("parallel"),
        name="inproj_sample",
    )(x, g, w_tokT, w_kvT, bf_col, gv)


FF_CHUNK = 256


def _merge_ffn_kernel(x_ref, u_ref, va_ref, b_ref, c_ref, wmix_ref, bmix_ref, gmix_ref, wo_ref,
                      gffn_ref, win_ref, wout_ref, gfin_ref, y_ref, *, tm, final_norm):
    lane_grp = lax.broadcasted_iota(jnp.int32, (CHUNK, W_A), 1) // HEAD_DIM
    a_rows = []
    for r in range(tm // CHUNK):
        va = va_ref[r * CHUNK:(r + 1) * CHUNK, :]
        mixed = bmix_ref[...]
        for g in range(N_A):
            vg = jnp.where(lane_grp == g, va, 0.0).astype(BF)
            mixed = mixed + _dot(wmix_ref[g], vg)
        a_rows.append(u_ref[r * CHUNK:(r + 1) * CHUNK, :] * mixed)
    a_out = jnp.concatenate(a_rows, axis=0) if len(a_rows) > 1 else a_rows[0]
    gm = gmix_ref[...]
    cat = jnp.concatenate([_rms(a_out, gm[:, 0:W_A]).astype(BF),
                           _rms(b_ref[...], gm[:, W_A:W_A + W_B]).astype(BF),
                           _rms(c_ref[...], gm[:, W_A + W_B:]).astype(BF)], axis=1)
    y = x_ref[...] + _dot(cat, wo_ref[...])
    yn = _rms(y, gffn_ref[...]).astype(BF)
    acc = y
    for c in range(D_FF // FF_CHUNK):
        h = _dot(yn, win_ref[:, c * FF_CHUNK:(c + 1) * FF_CHUNK])
        gate = _dot(yn, win_ref[:, D_FF + c * FF_CHUNK:D_FF + (c + 1) * FF_CHUNK])
        act = (h * jax.nn.sigmoid(h) * gate).astype(BF)
        acc = acc + _dot(act, wout_ref[c * FF_CHUNK:(c + 1) * FF_CHUNK, :])
    if final_norm:
        acc = _rms(acc, gfin_ref[...])
    y_ref[...] = acc


def _mix_operands_prompt(w_s, b_s):
    wmix = jnp.tril(w_s).astype(BF)
    bmix = jnp.repeat(jnp.swapaxes(b_s, 1, 2), HEAD_DIM, axis=2)
    return wmix, bmix


def _mix_operands_sample(w_s, b_s, t):
    depth = w_s.shape[0]
    reps = CHUNK // t
    small = jnp.tril(w_s[:, :, :t, :t])
    eye = jnp.eye(reps, dtype=small.dtype)
    wmix = jnp.einsum("rs,dgij->dgrisj", eye, small).reshape(depth, N_A, CHUNK, CHUNK).astype(BF)
    bmix = jnp.tile(jnp.repeat(jnp.swapaxes(b_s[:, :, :t], 1, 2), HEAD_DIM, axis=2), (1, reps, 1))
    return wmix, bmix


def _merge_ffn(x, u, va, b_out, c_out, wmix, bmix, gmix, wo, gffn, win, wout, gfin, *, layer, tm, final_norm):
    N, D = x.shape
    row = lambda w: pl.BlockSpec((tm, w), lambda i: (i, 0))
    params = (wmix, bmix, gmix, wo, gffn, win, wout)
    return pl.pallas_call(
        functools.partial(_merge_ffn_kernel, tm=tm, final_norm=final_norm),
        grid=(N // tm,),
        in_specs=[row(D), row(W_A), row(W_A), row(W_B), row(W_C)]
                 + [_layer_spec(a, layer) for a in params] + [_const_spec((1, D))],
        out_specs=row(D),
        out_shape=jax.ShapeDtypeStruct((N, D), F32),
        compiler_params=_cparams("parallel"),
        name="merge_ffn",
    )(x, u, va, b_out, c_out, *params, gfin)


def _split3(x):
    hi = x.astype(BF)
    r = x - hi.astype(F32)
    mid = r.astype(BF)
    lo = (r - mid.astype(F32)).astype(BF)
    return hi, mid, lo


def _fcum_kernel(lfr_ref, lfc_ref, fr_ref, fc_ref, *, S):
    ii = lax.broadcasted_iota(jnp.int32, (BLK, BLK), 0)
    jj = lax.broadcasted_iota(jnp.int32, (BLK, BLK), 1)
    t_incl = jnp.where(ii <= jj, 1.0, 0.0).astype(BF)
    l_incl = jnp.where(jj <= ii, 1.0, 0.0).astype(BF)
    rhs1 = jnp.concatenate([t_incl, jnp.ones((BLK, BLK), BF)], axis=1)
    rhs = jnp.concatenate([rhs1, rhs1, rhs1], axis=0)
    carry_r = jnp.zeros((SUBLANE, BLK), F32)
    carry_c = jnp.zeros((1, LANE), F32)
    for blk in range(S // BLK):
        sl = slice(blk * BLK, (blk + 1) * BLK)
        cs = _dot(jnp.concatenate(_split3(lfr_ref[:, sl]), axis=1), rhs)
        fr_ref[:, sl] = (cs[:, :BLK] + carry_r) * LOG2E
        carry_r = carry_r + cs[:, BLK:]
        hi, mid, lo = _split3(lfc_ref[sl, :])
        fc = _dot(l_incl, hi) + _dot(l_incl, mid) + _dot(l_incl, lo) + carry_c
        fc_ref[sl, :] = fc * LOG2E
        carry_c = fc[BLK - 1:BLK, :]


def _fcum(lf_row, lf_col, *, layer):
    _, B, _, S = lf_row.shape
    return pl.pallas_call(
        functools.partial(_fcum_kernel, S=S),
        grid=(B,),
        in_specs=[pl.BlockSpec((None, None, SUBLANE, S), lambda b: (layer, b, 0, 0)),
                  pl.BlockSpec((None, S, LANE), lambda b: (b, 0, 0))],
        out_specs=[pl.BlockSpec((None, SUBLANE, S), lambda b: (b, 0, 0)),
                   pl.BlockSpec((None, S, LANE), lambda b: (b, 0, 0))],
        out_shape=[jax.ShapeDtypeStruct((B, SUBLANE, S), F32), jax.ShapeDtypeStruct((B, S, LANE), F32)],
        compiler_params=_cparams("parallel"),
        name="fox_cumsum",
    )(lf_row, lf_col)


K_TILE_BLOCKS = 4
FOX_K_TILE_BLOCKS = 8


def _ktile(ref, h, start, width):
    return ref[h * HEAD_DIM:(h + 1) * HEAD_DIM, pl.ds(pl.multiple_of(start, BLK), width)]


def _fox_prompt_kernel(q_ref, kT_ref, vT_ref, fr_ref, fc_ref, o_ref, m_scr, acc_scr, fq_scr, *, kw):
    qi = pl.program_id(1)
    nblk = kw // BLK
    gd = qi // nblk
    causal = lax.broadcasted_iota(jnp.int32, (BLK, BLK), 1) <= lax.broadcasted_iota(jnp.int32, (BLK, BLK), 0)
    qs = [q_ref[:, h * HEAD_DIM:(h + 1) * HEAD_DIM].astype(BF) for h in range(N_C)]
    for h in range(N_C):
        fq_scr[h] = jnp.broadcast_to(fc_ref[:, h:h + 1], (BLK, LANE))

    def step(g, width, mask_last, first):
        ones_rows = jnp.ones((SUBLANE, width), BF)
        nb = width // BLK
        st = [dict() for _ in range(N_C)]
        for h, s in _issue_order(N_C, 2, FOX_SKEW):
            d = st[h]
            if s == 0:
                frow = fr_ref[h:h + 1, pl.ds(pl.multiple_of(g * kw, BLK), width)]
                qk = _dot(qs[h], _ktile(kT_ref, h, g * kw, width).astype(BF))
                fq = fq_scr[h]
                z = [qk[:, b * BLK:(b + 1) * BLK] + fq - frow[:, b * BLK:(b + 1) * BLK] for b in range(nb)]
                if mask_last is not None:
                    z[-1] = jnp.where(mask_last, z[-1], NEG)
                zmax = z[0]
                for zb in z[1:]:
                    zmax = jnp.maximum(zmax, zb)
                m_new = jnp.broadcast_to(jnp.max(zmax, axis=1, keepdims=True), (BLK, LANE))
                if not first:
                    m_old = m_scr[h]
                    m_new = jnp.maximum(m_old, m_new)
                    d["alpha"] = jnp.exp2(m_old - m_new)[:, :HEAD_DIM + SUBLANE]
                p = [jnp.exp2(zb - m_new).astype(BF) for zb in z]
                d["p"] = jnp.concatenate(p, axis=1) if nb > 1 else p[0]
                m_scr[h] = m_new
            else:
                v_ext = jnp.concatenate([_ktile(vT_ref, h, g * kw, width).astype(BF), ones_rows], axis=0)
                pv = _dot_nt(d["p"], v_ext)
                acc_scr[h] = pv if first else acc_scr[h] * d["alpha"] + pv
        return 0

    diag = [functools.partial(step, gd, (w + 1) * BLK, causal, True) for w in range(nblk)]
    if nblk > 1:
        lax.switch(qi % nblk, diag)
    else:
        diag[0]()
    lax.fori_loop(0, gd, lambda g, _: step(g, kw, None, False), 0)
    outs = []
    for h in range(N_C):
        acc = acc_scr[h]
        outs.append(acc[:, :HEAD_DIM] / acc[:, HEAD_DIM:HEAD_DIM + 1])
    o_ref[...] = jnp.concatenate(outs, axis=1)


def _fox_prompt(q, kT, vT, f_row, f_col, *, layer):
    B, S, W = q.shape
    kw = min(FOX_K_TILE_BLOCKS * BLK, S)
    return pl.pallas_call(
        functools.partial(_fox_prompt_kernel, kw=kw),
        grid=(B, S // BLK),
        in_specs=[pl.BlockSpec((None, BLK, W), lambda b, i: (b, i, 0)),
                  pl.BlockSpec((None, None, W, S), lambda b, i: (layer, b, 0, 0)),
                  pl.BlockSpec((None, None, W, S), lambda b, i: (layer, b, 0, 0)),
                  pl.BlockSpec((None, SUBLANE, S), lambda b, i: (b, 0, 0)),
                  pl.BlockSpec((None, BLK, LANE), lambda b, i: (b, i, 0))],
        out_specs=pl.BlockSpec((None, BLK, W), lambda b, i: (b, i, 0)),
        out_shape=jax.ShapeDtypeStruct((B, S, W), F32),
        scratch_shapes=[pltpu.VMEM((N_C, BLK, LANE), F32), pltpu.VMEM((N_C, BLK, HEAD_DIM + SUBLANE), F32),
                        pltpu.VMEM((N_C, BLK, LANE), F32)],
        compiler_params=_cparams("parallel", "arbitrary"),
        name="fox_prompt",
    )(q, kT, vT, f_row, f_col)


def _suffix_sum_rhs():
    k = lax.broadcasted_iota(jnp.int32, (2 * BLK, 2 * BLK), 0) % BLK
    n = lax.broadcasted_iota(jnp.int32, (2 * BLK, 2 * BLK), 1)
    return jnp.where((n >= BLK) | (k > n), 1.0, 0.0).astype(BF)


def _sb_pre(z, mask_last):
    nb = z.shape[1] // BLK
    sp = jnp.log(1.0 + jnp.exp2(-jnp.abs(z))) * LOG2E
    ls = jnp.minimum(z, 0.0) - sp
    l1m = ls - z
    blk = lambda x, b: x[:, b * BLK:(b + 1) * BLK]
    lhs = []
    for b in range(nb):
        lb = blk(l1m, b)
        if mask_last is not None and b == nb - 1:
            lb = jnp.where(mask_last, lb, 0.0)
        hi = lb.astype(BF)
        lo = (lb - hi.astype(F32)).astype(BF)
        lhs.append(jnp.concatenate([hi, lo], axis=1))
    return ls, (jnp.concatenate(lhs, axis=0) if nb > 1 else lhs[0])


def _sb_post(ls, cs, c, mask_last):
    r, n = ls.shape
    nb = n // BLK
    a = [None] * nb
    for b in reversed(range(nb)):
        csb = cs[b * r:(b + 1) * r]
        e = ls[:, b * BLK:(b + 1) * BLK] + csb[:, :BLK]
        ab = jnp.exp2(e if c is None else e + c)
        if mask_last is not None and b == nb - 1:
            ab = jnp.where(mask_last, ab, 0.0)
        a[b] = ab.astype(BF)
        c = csb[:, BLK:] if c is None else c + csb[:, BLK:]
    return (jnp.concatenate(a, axis=1) if nb > 1 else a[0]), c


def _issue_order(n_chains, n_stages, dist):
    items = [(h + s * dist, s, h) for h in range(n_chains) for s in range(n_stages)]
    return [(h, s) for _, s, h in sorted(items)]


SB_SKEW = 4
FOX_SKEW = 6


def _sb_prompt_kernel(q_ref, kT_ref, vT_ref, rhs_ref, o_ref, c_scr, acc_scr, *, kw):
    qi = pl.program_id(1)
    nblk = kw // BLK
    gd = qi // nblk
    strict = lax.broadcasted_iota(jnp.int32, (BLK, BLK), 1) < lax.broadcasted_iota(jnp.int32, (BLK, BLK), 0)
    qs = [q_ref[:, h * HEAD_DIM:(h + 1) * HEAD_DIM].astype(BF) for h in range(N_B)]

    def step(g, width, mask_last, first):
        st = [dict() for _ in range(N_B)]
        for h, s in _issue_order(N_B, 3, SB_SKEW):
            d = st[h]
            if s == 0:
                z = _dot(qs[h], _ktile(kT_ref, h, g * kw, width).astype(BF))
                d["ls"], d["lhs"] = _sb_pre(z, mask_last)
            elif s == 1:
                cs = _dot(d["lhs"], rhs_ref[...])
                d["a"], c = _sb_post(d["ls"], cs, None if first else c_scr[h], mask_last)
                c_scr[h] = c
            else:
                pv = _dot_nt(d["a"], _ktile(vT_ref, h, g * kw, width).astype(BF))
                acc_scr[h] = pv if first else acc_scr[h] + pv
        return 0

    diag = [functools.partial(step, gd, (w + 1) * BLK, strict, True) for w in range(nblk)]
    if nblk > 1:
        lax.switch(qi % nblk, diag)
    else:
        diag[0]()
    lax.fori_loop(0, gd, lambda i, _: step(gd - 1 - i, kw, None, False), 0)
    o_ref[...] = jnp.concatenate([acc_scr[h] for h in range(N_B)], axis=1)


def _sb_prompt(q, kT, vT, *, layer):
    B, S, W = q.shape
    kw = min(K_TILE_BLOCKS * BLK, S)
    return pl.pallas_call(
        functools.partial(_sb_prompt_kernel, kw=kw),
        grid=(B, S // BLK),
        in_specs=[pl.BlockSpec((None, BLK, W), lambda b, i: (b, i, 0)),
                  pl.BlockSpec((None, None, W, S), lambda b, i: (layer, b, 0, 0)),
                  pl.BlockSpec((None, None, W, S), lambda b, i: (layer, b, 0, 0)),
                  _const_spec((2 * BLK, 2 * BLK))],
        out_specs=pl.BlockSpec((None, BLK, W), lambda b, i: (b, i, 0)),
        out_shape=jax.ShapeDtypeStruct((B, S, W), F32),
        scratch_shapes=[pltpu.VMEM((N_B, BLK, BLK), F32), pltpu.VMEM((N_B, BLK, HEAD_DIM), F32)],
        compiler_params=_cparams("parallel", "arbitrary"),
        name="sb_prompt",
    )(q, kT, vT, _suffix_sum_rhs())


def _stack_heads(parts):
    return jnp.concatenate(parts, axis=0)


def _decode_kernel(pt_ref, qb_ref, qc_ref, kb_ref, vb_ref, kc_ref, vc_ref, lf_ref, *refs,
                   layer, n_pages, t_new, n_seq, seq=None):
    n = n_pages
    (sbk_hbm, sbv_hbm, fk_hbm, fv_hbm, flf_hbm, ob_ref, oc_ref,
     sbk_buf, sbv_buf, fk_buf, fv_buf, flf_buf, sem) = refs
    b = pl.program_id(0) if seq is None else seq
    slot = lax.rem(b, 2)

    def page_copies(seq, dst_slot):
        cps = []
        for p in range(n):
            pg = pt_ref[seq, p]
            for src, dst in ((sbk_hbm, sbk_buf), (sbv_hbm, sbv_buf), (fk_hbm, fk_buf), (fv_hbm, fv_buf),
                             (flf_hbm, flf_buf)):
                cps.append(pltpu.make_async_copy(src.at[layer, pg], dst.at[dst_slot, p], sem.at[dst_slot]))
        return cps

    @pl.when(b == 0)
    def _():
        for cp in page_copies(0, 0):
            cp.start()

    for cp in page_copies(b, slot):
        cp.wait()
    nxt = jnp.minimum(b + 1, n_seq - 1)
    for cp in page_copies(nxt, 1 - slot):
        cp.start()

    sbk = [sbk_buf.at[slot, p] for p in range(n)]
    sbv = [sbv_buf.at[slot, p] for p in range(n)]
    fk = [fk_buf.at[slot, p] for p in range(n)]
    fv = [fv_buf.at[slot, p] for p in range(n)]
    flf = [flf_buf.at[slot, p] for p in range(n)]
    T = t_new
    rows = N_B * T
    hs = lambda x, h: x[:, h * HEAD_DIM:(h + 1) * HEAD_DIM]

    sb_rhs = _suffix_sum_rhs()
    kk = lax.broadcasted_iota(jnp.int32, (3 * PAGE, 2 * PAGE), 0) % PAGE
    nn = lax.broadcasted_iota(jnp.int32, (3 * PAGE, 2 * PAGE), 1)
    lf_rhs = jnp.where((nn >= PAGE) | (kk > nn), 1.0, 0.0).astype(BF)
    t_idx = lax.broadcasted_iota(jnp.int32, (rows, PAGE), 0) % T
    s_idx = lax.broadcasted_iota(jnp.int32, (rows, PAGE), 1)
    zpad = jnp.zeros((PAGE - T, HEAD_DIM), BF)
    ones_rows = jnp.ones((SUBLANE, PAGE), BF)

    page = lambda x, p: x[:, p * PAGE:(p + 1) * PAGE]

    qb = qb_ref[...].astype(BF)
    kb_new = kb_ref[...].astype(BF)
    vb_new = vb_ref[...].astype(BF)

    lf_new = lf_ref[...]
    g_rows = [lf_new[0:1]]
    for t in range(1, T):
        g_rows.append(g_rows[-1] + lf_new[t:t + 1])
    g_col = jnp.concatenate(g_rows, axis=0)
    eye = lax.broadcasted_iota(jnp.int32, (T, PAGE), 0) == lax.broadcasted_iota(jnp.int32, (T, PAGE), 1)
    g_q = _stack_heads([jnp.broadcast_to(g_col[:, h:h + 1], (T, PAGE)) for h in range(N_C)])
    g_k = _stack_heads([jnp.broadcast_to(
        jnp.sum(jnp.where(eye, jnp.broadcast_to(g_col[:, h:h + 1], (T, PAGE)), 0.0), axis=0, keepdims=True),
        (T, PAGE)) for h in range(N_C)])
    qc = qc_ref[...].astype(BF)
    kc_new = kc_ref[...].astype(BF)
    vc_new = vc_ref[...].astype(BF)

    zb_new = _stack_heads([_dot_nt(hs(qb, h), jnp.concatenate([hs(kb_new, h), zpad], axis=0)) for h in range(N_B)])
    zb = jnp.concatenate([_stack_heads([_dot(hs(qb, h), sbk[p][h].astype(BF)) for h in range(N_B)])
                          for p in range(n)], axis=1)

    lf_pad = jnp.zeros((SUBLANE - N_C, PAGE), F32)
    lf_all = jnp.concatenate([jnp.concatenate([flf[p][...], lf_pad], axis=0) for p in range(n)], axis=0)
    cs = _dot(jnp.concatenate(_split3(lf_all), axis=1), lf_rhs)
    zc_new = _stack_heads([_dot_nt(hs(qc, h), jnp.concatenate([hs(kc_new, h), zpad], axis=0)) for h in range(N_C)])
    zc_pages = [_stack_heads([_dot(hs(qc, h), fk[p][h].astype(BF)) for h in range(N_C)]) for p in range(n)]

    mask_new = s_idx < t_idx
    ls_new, lhs_new = _sb_pre(zb_new, mask_new)
    ls_all, lhs_all = _sb_pre(zb, None)
    cs_new = _dot(lhs_new, sb_rhs)
    cs_all = _dot(lhs_all, sb_rhs)

    zc_new = jnp.where(s_idx <= t_idx, zc_new + g_q - g_k, NEG)
    r_carry = jnp.zeros((SUBLANE, PAGE), F32)
    for p in reversed(range(n)):
        csp = cs[p * SUBLANE:(p + 1) * SUBLANE]
        r_page = csp[:, :PAGE] + r_carry
        r_carry = r_carry + csp[:, PAGE:]
        bias = _stack_heads([jnp.broadcast_to(r_page[h:h + 1], (T, PAGE)) for h in range(N_C)]) + g_q
        zc_pages[p] = zc_pages[p] + bias
    zc = jnp.concatenate(zc_pages, axis=1)
    m = jnp.maximum(jnp.max(zc_new, axis=1, keepdims=True), jnp.max(zc, axis=1, keepdims=True))
    p_new = jnp.exp(zc_new - m)
    p_all = jnp.exp(zc - m)
    den = jnp.sum(p_new, axis=1, keepdims=True) + jnp.sum(p_all, axis=1, keepdims=True)
    p_new = p_new.astype(BF)
    p_all = p_all.astype(BF)
    accs = [_dot(p_new[h * T:(h + 1) * T], jnp.concatenate([hs(vc_new, h), zpad], axis=0)) for h in range(N_C)]
    for p in range(n):
        accs = [accs[h] + _dot_nt(page(p_all, p)[h * T:(h + 1) * T], fv[p][h].astype(BF)) for h in range(N_C)]

    a_new, c = _sb_post(ls_new, cs_new, None, mask_new)
    a_all, _ = _sb_post(ls_all, cs_all, c, None)
    acc = [_dot(a_new[h * T:(h + 1) * T], jnp.concatenate([hs(vb_new, h), zpad], axis=0)) for h in range(N_B)]
    for p in range(n):
        acc = [acc[h] + _dot_nt(page(a_all, p)[h * T:(h + 1) * T], sbv[p][h].astype(BF)) for h in range(N_B)]

    out = _stack_heads(accs) / den
    oc_ref[...] = jnp.concatenate([out[h * T:(h + 1) * T] for h in range(N_C)], axis=1)
    ob_ref[...] = jnp.concatenate(acc, axis=1)

    @pl.when(b == n_seq - 1)
    def _():
        for cp in page_copies(nxt, 1 - slot):
            cp.wait()


def _decode(page_table, qb, qc, kb, vb, kc, vc, lf, sbkT, sbvT, fkT, fvT, flfT, *, layer, t_new):
    nb, n_pages = page_table.shape
    N = qb.shape[0]
    tok = lambda w: pl.BlockSpec((t_new, w), lambda b, pt: (b, 0))
    hbm = pl.BlockSpec(memory_space=pl.ANY)
    kv_buf = pltpu.VMEM((2, n_pages, N_B, HEAD_DIM, PAGE), F32)
    return pl.pallas_call(
        functools.partial(_decode_kernel, layer=layer, n_pages=n_pages, t_new=t_new, n_seq=nb),
        grid_spec=pltpu.PrefetchScalarGridSpec(
            num_scalar_prefetch=1,
            grid=(nb,),
            in_specs=[tok(W_B), tok(W_C), tok(W_B), tok(W_B), tok(W_C), tok(W_C), tok(LANE)] + [hbm] * 5,
            out_specs=[tok(W_B), tok(W_C)],
            scratch_shapes=[kv_buf, kv_buf, kv_buf, kv_buf, pltpu.VMEM((2, n_pages, N_C, PAGE), F32),
                            pltpu.SemaphoreType.DMA((2,))],
        ),
        out_shape=[jax.ShapeDtypeStruct((N, W_B), F32), jax.ShapeDtypeStruct((N, W_C), F32)],
        compiler_params=_cparams("arbitrary"),
        name="decode_attention",
    )(page_table, qb, qc, kb, vb, kc, vc, lf, sbkT, sbvT, fkT, fvT, flfT)


def _sb_decode_kernel(pt_ref, q_ref, kT_ref, vT_ref, rhs_ref,
                      qb_ref, qc_ref, kb_ref, vb_ref, kc_ref, vc_ref, lf_ref,
                      sbk_hbm, sbv_hbm, fk_hbm, fv_hbm, flf_hbm,
                      o_ref, ob_ref, oc_ref,
                      c_scr, acc_scr, sbk_buf, sbv_buf, fk_buf, fv_buf, flf_buf, sem,
                      *, kw, layer, n_pages, t_new, n_seq, nq):
    seq = pl.program_id(0) * nq + pl.program_id(1)
    _decode_kernel(pt_ref, qb_ref, qc_ref, kb_ref, vb_ref, kc_ref, vc_ref, lf_ref,
                   sbk_hbm, sbv_hbm, fk_hbm, fv_hbm, flf_hbm, ob_ref, oc_ref,
                   sbk_buf, sbv_buf, fk_buf, fv_buf, flf_buf, sem,
                   layer=layer, n_pages=n_pages, t_new=t_new, n_seq=n_seq, seq=seq)
    _sb_prompt_kernel(q_ref, kT_ref, vT_ref, rhs_ref, o_ref, c_scr, acc_scr, kw=kw)


def _sb_prompt_decode(q, kT, vT, page_table, qb, qc, kb, vb, kc, vc, lf, sbkT, sbvT, fkT, fvT, flfT,
                      *, layer, t_new):
    B, S, W = q.shape
    nq = S // BLK
    n_seq, n_pages = page_table.shape
    assert B * nq == n_seq, "one decode sequence per prompt query block"
    N = qb.shape[0]
    kw = min(K_TILE_BLOCKS * BLK, S)
    tok = lambda w: pl.BlockSpec((t_new, w), lambda b, i, pt: (b * nq + i, 0))
    hbm = pl.BlockSpec(memory_space=pl.ANY)
    kv_buf = pltpu.VMEM((2, n_pages, N_B, HEAD_DIM, PAGE), F32)
    return pl.pallas_call(
        functools.partial(_sb_decode_kernel, kw=kw, layer=layer, n_pages=n_pages, t_new=t_new, n_seq=n_seq, nq=nq),
        grid_spec=pltpu.PrefetchScalarGridSpec(
            num_scalar_prefetch=1,
            grid=(B, nq),
            in_specs=[pl.BlockSpec((None, BLK, W), lambda b, i, pt: (b, i, 0)),
                      pl.BlockSpec((None, None, W, S), lambda b, i, pt: (layer, b, 0, 0)),
                      pl.BlockSpec((None, None, W, S), lambda b, i, pt: (layer, b, 0, 0)),
                      _const_spec((2 * BLK, 2 * BLK)),
                      tok(W_B), tok(W_C), tok(W_B), tok(W_B), tok(W_C), tok(W_C), tok(LANE)] + [hbm] * 5,
            out_specs=[pl.BlockSpec((None, BLK, W), lambda b, i, pt: (b, i, 0)), tok(W_B), tok(W_C)],
            scratch_shapes=[pltpu.VMEM((N_B, BLK, BLK), F32), pltpu.VMEM((N_B, BLK, HEAD_DIM), F32),
                            kv_buf, kv_buf, kv_buf, kv_buf, pltpu.VMEM((2, n_pages, N_C, PAGE), F32),
                            pltpu.SemaphoreType.DMA((2,))],
        ),
        out_shape=[jax.ShapeDtypeStruct((B, S, W), F32),
                   jax.ShapeDtypeStruct((N, W_B), F32), jax.ShapeDtypeStruct((N, W_C), F32)],
        compiler_params=_cparams("arbitrary", "arbitrary"),
        name="sb_prompt_decode",
    )(page_table, q, kT, vT, _suffix_sum_rhs(), qb, qc, kb, vb, kc, vc, lf, sbkT, sbvT, fkT, fvT, flfT)


TM_PROMPT = 512
TM_SAMPLE = 512


def kernel(x_prompt, x_sample, cache_sb_k, cache_sb_v, cache_fox_k, cache_fox_v, cache_fox_logf, page_table, g_attn, w_in, b_f, g_v, w_s, b_s, g_mix, w_o, g_ffn, w_ffn_in, w_ffn_out, g_final):
    B, S, D = x_prompt.shape
    DB, T, _ = x_sample.shape
    depth = g_attn.shape[0]
    assert D == D_MODEL and S % TM_PROMPT == 0 and (DB * T) % TM_SAMPLE == 0
    assert CHUNK % T == 0 and TM_SAMPLE % CHUNK == 0 and cache_sb_k.shape[2] == PAGE

    kv_t = lambda c: jnp.transpose(c, (0, 1, 3, 4, 2))
    sbkT, sbvT, fkT, fvT = kv_t(cache_sb_k), kv_t(cache_sb_v), kv_t(cache_fox_k), kv_t(cache_fox_v)
    flfT = jnp.transpose(cache_fox_logf, (0, 1, 3, 2))

    yp = x_prompt
    ys = x_sample.reshape(DB * T, D)
    stacked = None
    outs_s = [[] for _ in range(6)]

    w_tokT, w_kvT, bf_col, bf_row = _split_w_in(w_in, b_f)
    wo, wfi, wfo = w_o.astype(BF), w_ffn_in.astype(BF), w_ffn_out.astype(BF)
    vec = lambda a: a.reshape(depth, 1, a.shape[-1])
    ga, gv, gm, gf, gfin = vec(g_attn), vec(g_v), vec(g_mix), vec(g_ffn), g_final.reshape(1, D)
    wmix_p, bmix_p = _mix_operands_prompt(w_s, b_s)
    wmix_s, bmix_s = _mix_operands_sample(w_s, b_s, T)

    for l in range(depth):
        last = l == depth - 1

        u, va, qb, qc, lfc, *stacked = _inproj_prompt(
            yp, ga, w_tokT, w_kvT, bf_col, bf_row, gv, stacked, layer=l, tm=TM_PROMPT)
        kbT, vbT, kcT, vcT, lfr = stacked
        f_row, f_col = _fcum(lfr, lfc, layer=l)
        c_out = _fox_prompt(qc, kcT, vcT, f_row, f_col, layer=l)

        us, vas, qbs, qcs, lfcs, kb, vb, kc, vc = _inproj_sample(
            ys, ga, w_tokT, w_kvT, bf_col, gv, layer=l, tm=TM_SAMPLE)
        b_out, bs_out, cs_out = _sb_prompt_decode(qb, kbT, vbT, page_table, qbs, qcs, kb, vb, kc, vc, lfcs,
                                                  sbkT, sbvT, fkT, fvT, flfT, layer=l, t_new=T)

        flat = lambda a: a.reshape(B * S, a.shape[-1])
        yp = _merge_ffn(flat(yp), flat(u), flat(va), flat(b_out), flat(c_out), wmix_p, bmix_p, gm, wo,
                        gf, wfi, wfo, gfin, layer=l, tm=TM_PROMPT, final_norm=last).reshape(B, S, D)
        ys = _merge_ffn(ys, us, vas, bs_out, cs_out, wmix_s, bmix_s, gm, wo,
                        gf, wfi, wfo, gfin, layer=l, tm=TM_SAMPLE, final_norm=last)
        lfc, va = lfcs, vas
        for dst, a in zip(outs_s, (kb, vb, kc, vc)):
            dst.append(a.reshape(DB, T, N_B, HEAD_DIM))
        outs_s[4].append(lfc[:, :N_C].reshape(DB, T, N_C))
        outs_s[5].append(va.reshape(DB, T, W_A))

    to_seq_major = lambda a: jnp.transpose(a.reshape(depth, B, -1, HEAD_DIM, S), (0, 1, 4, 2, 3))
    kbT, vbT, kcT, vcT, lfr = stacked
    p_lf = jnp.transpose(lfr[:, :, :N_C, :], (0, 1, 3, 2))
    return (yp, ys.reshape(DB, T, D),
            to_seq_major(kbT), to_seq_major(vbT), to_seq_major(kcT), to_seq_major(vcT), p_lf,
            jnp.stack(outs_s[0]), jnp.stack(outs_s[1]), jnp.stack(outs_s[2]), jnp.stack(outs_s[3]),
            jnp.stack(outs_s[4]), jnp.stack(outs_s[5]))
```
